```python
import jax, jax.numpy as jnp
from jax import lax
import numpy as np

D_MODEL = 1024
BATCH = 4
SEQ = 4096
DEPTH = 4
DEC_BATCH = 128
DEC_SEQ = 4
PAST_LEN = 2048
PAGE_SIZE = 128

HEAD_DIM = 64
D_MIX = D_MODEL
D_FOX = D_MIX // 2
H_FOX = D_FOX // HEAD_DIM
D_GMLP = D_MIX // 4
G_GMLP = D_GMLP // HEAD_DIM
D_XATTN = D_MIX - D_FOX - D_GMLP
H_XATTN = D_XATTN // HEAD_DIM
N_MEM = 256
CHUNK = 128
Q_BLOCK = 128
FORGET_BIAS = 3.0
EPS = 1e-6
NEG = -1e30
D_IN = 3 * D_FOX + D_XATTN + 2 * D_GMLP + D_MIX + H_FOX
SPLIT_POINTS = (D_FOX, 2 * D_FOX, 3 * D_FOX, 3 * D_FOX + D_XATTN,
                3 * D_FOX + D_XATTN + D_GMLP, 3 * D_FOX + D_XATTN + 2 * D_GMLP,
                3 * D_FOX + D_XATTN + 2 * D_GMLP + D_MIX)

kernel_name = "hymba_fox_gmlp_memory_decoder_step"


def rmsnorm(x, gain):
    x32 = x.astype(jnp.float32)
    y = x32 * lax.rsqrt(jnp.mean(x32 * x32, axis=-1, keepdims=True) + EPS)
    return (y * gain.astype(jnp.float32)).astype(x.dtype)


def project(x, norm_pre, w_in, b_forget, q_norm, k_norm, v_norm):
    b, t = x.shape[:2]
    h = rmsnorm(x, norm_pre)
    z = jnp.einsum('btd,de->bte', h, w_in)
    q, k, v, qx, u, vg, gate, fl = jnp.split(z, SPLIT_POINTS, axis=-1)
    q = rmsnorm(q.reshape(b, t, H_FOX, HEAD_DIM), q_norm)
    k = rmsnorm(k.reshape(b, t, H_FOX, HEAD_DIM), k_norm)
    v = v.reshape(b, t, H_FOX, HEAD_DIM)
    qx = qx.reshape(b, t, H_XATTN, HEAD_DIM)
    vg = rmsnorm(vg, v_norm)
    lf = jax.nn.log_sigmoid(fl.astype(jnp.float32) + b_forget.astype(jnp.float32))
    return q, k, v, lf, qx, u, vg, gate


def fox_attend(q, k, v, c_q, c_k, q_pos, k_pos):
    s = jnp.einsum('bqhd,bkhd->bhqk', q.astype(jnp.float32), k.astype(jnp.float32)) * (HEAD_DIM ** -0.5)
    s = s + (jnp.swapaxes(c_q, 1, 2)[:, :, :, None] - jnp.swapaxes(c_k, 1, 2)[:, :, None, :])
    s = jnp.where(k_pos[None, None, None, :] <= q_pos[None, None, :, None], s, NEG)
    p = jax.nn.softmax(s, axis=-1)
    return jnp.einsum('bhqk,bkhd->bqhd', p, v.astype(jnp.float32)).astype(v.dtype)


def fox_prompt(q, k, v, lf):
    b, s = q.shape[:2]
    c = jnp.cumsum(lf, axis=1)
    nb = s // Q_BLOCK
    qb = jnp.swapaxes(q.reshape(b, nb, Q_BLOCK, H_FOX, HEAD_DIM), 0, 1)
    cb = jnp.swapaxes(c.reshape(b, nb, Q_BLOCK, H_FOX), 0, 1)
    starts = jnp.arange(nb, dtype=jnp.int32) * Q_BLOCK
    k_pos = jnp.arange(s, dtype=jnp.int32)

    def one_block(args):
        qi, ci, st = args
        return fox_attend(qi, k, v, ci, c, st + jnp.arange(Q_BLOCK, dtype=jnp.int32), k_pos)

    o = lax.map(one_block, (qb, cb, starts))
    return jnp.swapaxes(o, 0, 1).reshape(b, s, D_FOX)


def fox_sample(q, k, v, lf, k_past, v_past, lf_past):
    b, t = q.shape[:2]
    p_len = k_past.shape[1]
    k_all = jnp.concatenate([k_past.astype(k.dtype), k], axis=1)
    v_all = jnp.concatenate([v_past.astype(v.dtype), v], axis=1)
    c_all = jnp.cumsum(jnp.concatenate([lf_past.astype(jnp.float32), lf], axis=1), axis=1)
    q_pos = p_len + jnp.arange(t, dtype=jnp.int32)
    k_pos = jnp.arange(p_len + t, dtype=jnp.int32)
    o = fox_attend(q, k_all, v_all, c_all[:, p_len:], c_all, q_pos, k_pos)
    return o.reshape(b, t, D_FOX)


def gmlp_prompt(u, vg, w_s, b_s):
    b, s = u.shape[:2]
    w = w_s * jnp.tril(jnp.ones((CHUNK, CHUNK), w_s.dtype))
    vc = vg.reshape(b, s // CHUNK, CHUNK, G_GMLP, HEAD_DIM)
    mixed = jnp.einsum('gij,bcjgd->bcigd', w, vc) + b_s.T[None, None, :, :, None]
    return u * mixed.reshape(b, s, D_GMLP)


def gmlp_sample(u, vg, w_s, b_s):
    b, t = u.shape[:2]
    w = (w_s * jnp.tril(jnp.ones((CHUNK, CHUNK), w_s.dtype)))[:, :t, :t]
    mixed = jnp.einsum('gij,bjgd->bigd', w, vg.reshape(b, t, G_GMLP, HEAD_DIM)) + b_s[:, :t].T[None, :, :, None]
    return u * mixed.reshape(b, t, D_GMLP)


def memory_kv(mem, mem_norm, w_mem_kv):
    b = mem.shape[0]
    kv = jnp.einsum('bmd,de->bme', rmsnorm(mem, mem_norm), w_mem_kv)
    mk, mv = jnp.split(kv, 2, axis=-1)
    return mk.reshape(b, N_MEM, H_XATTN, HEAD_DIM), mv.reshape(b, N_MEM, H_XATTN, HEAD_DIM)


def cross_attend(qx, mk, mv):
    b, t = qx.shape[:2]
    s = jnp.einsum('bqhd,bmhd->bhqm', qx.astype(jnp.float32), mk.astype(jnp.float32)) * (HEAD_DIM ** -0.5)
    p = jax.nn.softmax(s, axis=-1)
    o = jnp.einsum('bhqm,bmhd->bqhd', p, mv.astype(jnp.float32)).astype(qx.dtype)
    return o.reshape(b, t, D_XATTN)


def finish(x, fox_o, gm_o, x_o, gate, branch_norm, w_out, norm_post):
    merged = jnp.concatenate([
        rmsnorm(fox_o, branch_norm[:D_FOX]),
        rmsnorm(gm_o, branch_norm[D_FOX:D_FOX + D_GMLP]),
        rmsnorm(x_o, branch_norm[D_FOX + D_GMLP:]),
    ], axis=-1) * jax.nn.silu(gate)
    y = jnp.einsum('bte,ed->btd', merged, w_out)
    return x + rmsnorm(y, norm_post)


def setup_inputs(seed: int = 0) -> dict:
    key = jax.random.key(seed)
    ks = jax.random.split(key, 24)
    f32 = jnp.float32
    n_pages = PAST_LEN // PAGE_SIZE
    used = DEC_BATCH * n_pages
    n_pool = used + max(1, used // 4)

    def nrm(k, shape, scale=1.0):
        return scale * jax.random.normal(k, shape, f32)

    page_table = jax.random.permutation(ks[8], n_pool)[:used].reshape(DEC_BATCH, n_pages).astype(jnp.int32)
    return {
        'x_prompt': nrm(ks[0], (BATCH, SEQ, D_MODEL)),
        'x_sample': nrm(ks[1], (DEC_BATCH, DEC_SEQ, D_MODEL)),
        'mem_prompt': nrm(ks[2], (BATCH, N_MEM, D_MODEL)),
        'cache_fox_k': nrm(ks[3], (DEPTH, n_pool, PAGE_SIZE, H_FOX, HEAD_DIM)),
        'cache_fox_v': nrm(ks[4], (DEPTH, n_pool, PAGE_SIZE, H_FOX, HEAD_DIM)),
        'cache_fox_lf': jax.nn.log_sigmoid(FORGET_BIAS + nrm(ks[5], (DEPTH, n_pool, PAGE_SIZE, H_FOX))),
        'cache_mem_k': nrm(ks[6], (DEPTH, DEC_BATCH, N_MEM, H_XATTN, HEAD_DIM)),
        'cache_mem_v': nrm(ks[7], (DEPTH, DEC_BATCH, N_MEM, H_XATTN, HEAD_DIM)),
        'page_table': page_table,
        'norm_pre': 1.0 + nrm(ks[9], (DEPTH, D_MODEL), 0.05),
        'w_in': nrm(ks[10], (DEPTH, D_MODEL, D_IN), D_MODEL ** -0.5),
        'b_forget': FORGET_BIAS + nrm(ks[11], (DEPTH, H_FOX), 0.1),
        'q_norm': 1.0 + nrm(ks[12], (DEPTH, HEAD_DIM), 0.05),
        'k_norm': 1.0 + nrm(ks[13], (DEPTH, HEAD_DIM), 0.05),
        'gmlp_v_norm': 1.0 + nrm(ks[14], (DEPTH, D_GMLP), 0.05),
        'w_spatial': nrm(ks[15], (DEPTH, G_GMLP, CHUNK, CHUNK), CHUNK ** -0.5),
        'b_spatial': 1.0 + nrm(ks[16], (DEPTH, G_GMLP, CHUNK), 0.02),
        'mem_norm': 1.0 + nrm(ks[17], (DEPTH, D_MODEL), 0.05),
        'w_mem_kv': nrm(ks[18], (DEPTH, D_MODEL, 2 * D_XATTN), D_MODEL ** -0.5),
        'branch_norm': 1.0 + nrm(ks[19], (DEPTH, D_MIX), 0.05),
        'w_out': nrm(ks[20], (DEPTH, D_MIX, D_MODEL), D_MIX ** -0.5),
        'norm_post': 1.0 + nrm(ks[21], (DEPTH, D_MODEL), 0.05),
    }


def reference(x_prompt, x_sample, mem_prompt, cache_fox_k, cache_fox_v, cache_fox_lf,
              cache_mem_k, cache_mem_v, page_table, norm_pre, w_in, b_forget, q_norm, k_norm,
              gmlp_v_norm, w_spatial, b_spatial, mem_norm, w_mem_kv, branch_norm, w_out, norm_post):
    xp, xs = x_prompt, x_sample
    dbatch = xs.shape[0]
    past = page_table.shape[1] * PAGE_SIZE
    fkp, fvp, flp, mkp, mvp = [], [], [], [], []
    fks, fvs, fls, gvs = [], [], [], []
    for l in range(DEPTH):
        q, k, v, lf, qx, u, vg, gate = project(xp, norm_pre[l], w_in[l], b_forget[l],
                                               q_norm[l], k_norm[l], gmlp_v_norm[l])
        mk, mv = memory_kv(mem_prompt, mem_norm[l], w_mem_kv[l])
        fo = fox_prompt(q, k, v, lf)
        go = gmlp_prompt(u, vg, w_spatial[l], b_spatial[l])
        xo = cross_attend(qx, mk, mv)
        xp = finish(xp, fo, go, xo, gate, branch_norm[l], w_out[l], norm_post[l])
        fkp.append(k); fvp.append(v); flp.append(lf); mkp.append(mk); mvp.append(mv)

        q, k, v, lf, qx, u, vg, gate = project(xs, norm_pre[l], w_in[l], b_forget[l],
                                               q_norm[l], k_norm[l], gmlp_v_norm[l])
        k_past = cache_fox_k[l][page_table].reshape(dbatch, past, H_FOX, HEAD_DIM)
        v_past = cache_fox_v[l][page_table].reshape(dbatch, past, H_FOX, HEAD_DIM)
        lf_past = cache_fox_lf[l][page_table].reshape(dbatch, past, H_FOX)
        fo = fox_sample(q, k, v, lf, k_past, v_past, lf_past)
        go = gmlp_sample(u, vg, w_spatial[l], b_spatial[l])
        xo = cross_attend(qx, cache_mem_k[l], cache_mem_v[l])
        xs = finish(xs, fo, go, xo, gate, branch_norm[l], w_out[l], norm_post[l])
        fks.append(k); fvs.append(v); fls.append(lf); gvs.append(vg)

    return (xp, xs,
            jnp.stack(fkp), jnp.stack(fvp), jnp.stack(flp), jnp.stack(mkp), jnp.stack(mvp),
            jnp.stack(fks), jnp.stack(fvs), jnp.stack(fls), jnp.stack(gvs))
```

```python
import functools

import jax
import jax.numpy as jnp
from jax import lax
from jax.experimental import pallas as pl
from jax.experimental.pallas import tpu as pltpu

HEAD_DIM = 64
H_FOX = 8
D_FOX = H_FOX * HEAD_DIM
G_GMLP = 4
D_GMLP = G_GMLP * HEAD_DIM
H_XATTN = 4
D_XATTN = H_XATTN * HEAD_DIM
D_MIX = D_FOX + D_GMLP + D_XATTN
N_MEM = 256
CHUNK = 128
PAGE_SIZE = 128
EPS = 1e-6
NEG = -1e30
SCALE = HEAD_DIM ** -0.5
LANES = 128
AUG = 2 * HEAD_DIM
D_REST = D_XATTN + 2 * D_GMLP + D_MIX
VMEM_LIMIT = 56 * 1024 * 1024

F32 = jnp.float32
BF16 = jnp.bfloat16

_NN = (((1,), (0,)), ((), ()))
_NT = (((1,), (1,)), ((), ()))
_TN = (((0,), (0,)), ((), ()))


def _dot(a, b, dims=_NN):
    return lax.dot_general(a, b, dims, preferred_element_type=F32)


def _split3(x):
    hi = x.astype(BF16).astype(F32)
    r = x - hi
    mid = r.astype(BF16).astype(F32)
    lo = (r - mid).astype(BF16).astype(F32)
    return hi, mid, lo


def _log_sigmoid(x):
    return jnp.minimum(x, 0.0) - jnp.log1p(jnp.exp(-jnp.abs(x)))


def _rms(x, gain):
    return x * lax.rsqrt(jnp.mean(x * x, axis=-1, keepdims=True) + EPS) * gain


def _cumsum_lanes(xT, tri):
    hi, mid, lo = _split3(xT)
    parts = jnp.concatenate([hi, mid, lo, jnp.zeros_like(hi)], axis=0).astype(BF16)
    cs = _dot(parts, tri)
    return cs[0:8] + cs[8:16] + cs[16:24]


def _proj_prompt_body(x_ref, gpre_ref, wa_ref, wkvT_ref, wflT_ref, bf_ref, qg_ref, kg_ref, vgain_ref,
                      gsum_ref, place_ref, selq_ref, selk_ref, tri_ref,
                      qa_ref, kTa_ref, vTa_ref, kT_ref, vT_ref, lfT_ref, rest_ref, carry_ref):
    tm = x_ref.shape[1]

    @pl.when(pl.program_id(1) == 0)
    def _():
        carry_ref[...] = jnp.zeros_like(carry_ref)

    x = x_ref[0]
    h = _rms(x, gpre_ref[...]).astype(BF16)

    za = _dot(h, wa_ref[...])
    q = za[:, :D_FOX]
    q2 = q * q
    q2h = q2.astype(BF16)
    q2l = (q2 - q2h.astype(F32)).astype(BF16)
    msq = (_dot(q2h, gsum_ref[...]) + _dot(q2l, gsum_ref[...])) * (1.0 / HEAD_DIM)
    qn = (q * lax.rsqrt(msq + EPS) * qg_ref[...]).astype(BF16)

    zkv = _dot(wkvT_ref[...], h, _NT)
    flT = _dot(wflT_ref[...], h, _NT)
    lfT = _log_sigmoid(flT[0:8] + bf_ref[...])
    lfT_ref[0] = lfT

    cT = _cumsum_lanes(lfT, tri_ref[...]) + carry_ref[:, 0:1]
    carry_ref[...] = jnp.broadcast_to(cT[:, tm - 1:tm], carry_ref.shape)
    chi, cmid, clo = _split3(cT)
    cparts = jnp.concatenate([chi, cmid, clo, jnp.ones_like(chi)], axis=0).astype(BF16)

    qa = _dot(qn, place_ref[...]) + _dot(cparts, selq_ref[...], _TN)
    qa = qa.astype(BF16)
    kaugT = _dot(selk_ref[...], cparts)
    row = lax.broadcasted_iota(jnp.int32, (HEAD_DIM, tm), 0)
    vaugT = jnp.where(row == 0, 1.0, 0.0).astype(F32)

    kg = kg_ref[...]
    for hh in range(H_FOX):
        sl = slice(hh * HEAD_DIM, (hh + 1) * HEAD_DIM)
        qa_ref[0, hh] = qa[:, hh * AUG:(hh + 1) * AUG]
        kh = zkv[sl]
        kn = kh * lax.rsqrt(jnp.mean(kh * kh, axis=0, keepdims=True) + EPS) * kg
        kT_ref[0, hh] = kn
        kTa_ref[0, hh] = jnp.concatenate([kn, kaugT[sl]], axis=0).astype(BF16)
        vh = zkv[D_FOX + hh * HEAD_DIM:D_FOX + (hh + 1) * HEAD_DIM]
        vT_ref[0, hh] = vh
        vTa_ref[0, hh] = jnp.concatenate([vh, vaugT], axis=0).astype(BF16)

    rest_ref[0, :, 0:D_XATTN + D_GMLP] = za[:, D_FOX:D_FOX + D_XATTN + D_GMLP]
    vg = za[:, D_FOX + D_XATTN + D_GMLP:D_FOX + D_XATTN + 2 * D_GMLP]
    rest_ref[0, :, D_XATTN + D_GMLP:D_XATTN + 2 * D_GMLP] = _rms(vg, vgain_ref[...])
    rest_ref[0, :, D_XATTN + 2 * D_GMLP:] = za[:, D_FOX + D_XATTN + 2 * D_GMLP:]


def _proj_prompt(x, w, consts, tm):
    b, s, d = x.shape
    n_t = s // tm
    full = lambda shape: pl.BlockSpec(shape, lambda bi, ti: (0,) * len(shape))
    in_specs = [
        pl.BlockSpec((1, tm, d), lambda bi, ti: (bi, ti, 0)),
        full((1, d)), full(w['wa'].shape), full(w['wkvT'].shape), full(w['wflT'].shape),
        full((8, 1)), full((1, D_FOX)), full((HEAD_DIM, 1)), full((1, D_GMLP)),
        full(consts['gsum'].shape), full(consts['place'].shape), full(consts['selq'].shape),
        full(consts['selk'].shape), full(consts['tri'].shape),
    ]
    out_shape = [
        jax.ShapeDtypeStruct((b, H_FOX, s, AUG), BF16),
        jax.ShapeDtypeStruct((b, H_FOX, AUG, s), BF16),
        jax.ShapeDtypeStruct((b, H_FOX, AUG, s), BF16),
        jax.ShapeDtypeStruct((b, H_FOX, HEAD_DIM, s), F32),
        jax.ShapeDtypeStruct((b, H_FOX, HEAD_DIM, s), F32),
        jax.ShapeDtypeStruct((b, H_FOX, s), F32),
        jax.ShapeDtypeStruct((b, s, D_REST), F32),
    ]
    out_specs = [
        pl.BlockSpec((1, H_FOX, tm, AUG), lambda bi, ti: (bi, 0, ti, 0)),
        pl.BlockSpec((1, H_FOX, AUG, tm), lambda bi, ti: (bi, 0, 0, ti)),
        pl.BlockSpec((1, H_FOX, AUG, tm), lambda bi, ti: (bi, 0, 0, ti)),
        pl.BlockSpec((1, H_FOX, HEAD_DIM, tm), lambda bi, ti: (bi, 0, 0, ti)),
        pl.BlockSpec((1, H_FOX, HEAD_DIM, tm), lambda bi, ti: (bi, 0, 0, ti)),
        pl.BlockSpec((1, H_FOX, tm), lambda bi, ti: (bi, 0, ti)),
        pl.BlockSpec((1, tm, D_REST), lambda bi, ti: (bi, ti, 0)),
    ]
    return pl.pallas_call(
        _proj_prompt_body,
        grid=(b, n_t),
        in_specs=in_specs, out_specs=out_specs, out_shape=out_shape,
        scratch_shapes=[pltpu.VMEM((8, LANES), F32)],
        compiler_params=pltpu.CompilerParams(
            dimension_semantics=("arbitrary", "arbitrary"), vmem_limit_bytes=VMEM_LIMIT),
        name="proj_prompt",
    )(x, w['gpre'], w['wa'], w['wkvT'], w['wflT'], w['bf'], w['qg'], w['kg'], w['vgain'],
      consts['gsum'], consts['place'], consts['selq'], consts['selk'], consts['tri'])


def _fox_prompt_body(qa_ref, kTa_ref, vTa_ref, o_ref, m_ref, acc_ref):
    bq = qa_ref.shape[2]
    bk = kTa_ref.shape[3]
    i = pl.program_id(1)
    j = pl.program_id(2)

    @pl.when(j == 0)
    def _():
        m_ref[...] = jnp.full_like(m_ref, NEG)
        acc_ref[...] = jnp.zeros_like(acc_ref)

    @pl.when(j <= i)
    def _():
        rows = i * bq + lax.broadcasted_iota(jnp.int32, (bq, bk), 0)
        cols = j * bk + lax.broadcasted_iota(jnp.int32, (bq, bk), 1)
        keep = cols <= rows

        def head(hh, carry):
            s = _dot(qa_ref[0, hh], kTa_ref[0, hh])
            s = jnp.where(keep, s, NEG)
            m_prev = m_ref[hh][:, 0:1]
            m_new = jnp.maximum(m_prev, jnp.max(s, axis=-1, keepdims=True))
            alpha = jnp.exp(m_prev - m_new)
            p = jnp.exp(s - m_new).astype(BF16)
            pv = _dot(p, vTa_ref[0, hh], _NT)
            acc_ref[hh] = alpha * acc_ref[hh] + pv
            m_ref[hh] = jnp.broadcast_to(m_new, (bq, LANES))
            return carry

        lax.fori_loop(0, H_FOX, head, 0)

    @pl.when(j == i)
    def _():
        lane = lax.broadcasted_iota(jnp.int32, (bq, LANES), 1)
        for pair in range(H_FOX // 2):
            a0 = acc_ref[2 * pair]
            a1 = acc_ref[2 * pair + 1]
            o0 = a0 / a0[:, HEAD_DIM:HEAD_DIM + 1]
            o1 = a1 / a1[:, HEAD_DIM:HEAD_DIM + 1]
            o_ref[0, :, pair * LANES:(pair + 1) * LANES] = jnp.where(
                lane < HEAD_DIM, o0, pltpu.roll(o1, HEAD_DIM, 1))


def _fox_prompt(qa, kTa, vTa, bq, bk):
    b, _, s, _ = qa.shape
    assert bq == bk
    nq, nk = s // bq, s // bk
    return pl.pallas_call(
        _fox_prompt_body,
        grid=(b, nq, nk),
        in_specs=[
            pl.BlockSpec((1, H_FOX, bq, AUG), lambda bi, i, j: (bi, 0, i, 0)),
            pl.BlockSpec((1, H_FOX, AUG, bk), lambda bi, i, j: (bi, 0, 0, jnp.minimum(i, j))),
            pl.BlockSpec((1, H_FOX, AUG, bk), lambda bi, i, j: (bi, 0, 0, jnp.minimum(i, j))),
        ],
        out_specs=pl.BlockSpec((1, bq, D_FOX), lambda bi, i, j: (bi, i, 0)),
        out_shape=jax.ShapeDtypeStruct((b, s, D_FOX), F32),
        scratch_shapes=[pltpu.VMEM((H_FOX, bq, LANES), F32), pltpu.VMEM((H_FOX, bq, AUG), F32)],
        compiler_params=pltpu.CompilerParams(
            dimension_semantics=("arbitrary", "arbitrary", "arbitrary"), vmem_limit_bytes=VMEM_LIMIT),
        name="fox_prompt",
    )(qa, kTa, vTa)


def _mem_kv_body(mem_ref, gain_ref, wT_ref, mkT_ref, mvT_ref, mkbd_ref, mvbd_ref):
    hm = _rms(mem_ref[0], gain_ref[0]).astype(BF16)
    kvT = _dot(wT_ref[0], hm, _NT)
    mkT = kvT[:D_XATTN]
    mvT = kvT[D_XATTN:]
    mkT_ref[0, 0] = mkT
    mvT_ref[0, 0] = mvT
    mkbd_ref[...] = jnp.zeros_like(mkbd_ref)
    mvbd_ref[...] = jnp.zeros_like(mvbd_ref)
    for hh in range(H_XATTN):
        r = slice(hh * HEAD_DIM, (hh + 1) * HEAD_DIM)
        c = slice(hh * N_MEM, (hh + 1) * N_MEM)
        mkbd_ref[0, 0, r, c] = mkT[r].astype(BF16)
        mvbd_ref[0, 0, r, c] = mvT[r].astype(BF16)


def _mem_kv(mem, mem_norm, wmT):
    b, n_mem, d = mem.shape
    depth = wmT.shape[0]
    blk = lambda shape: pl.BlockSpec((1, 1) + shape, lambda l, bi: (l, bi, 0, 0))
    return pl.pallas_call(
        _mem_kv_body,
        grid=(depth, b),
        in_specs=[
            pl.BlockSpec((1, n_mem, d), lambda l, bi: (bi, 0, 0)),
            pl.BlockSpec((1, 1, d), lambda l, bi: (l, 0, 0)),
            pl.BlockSpec((1, 2 * D_XATTN, d), lambda l, bi: (l, 0, 0)),
        ],
        out_specs=[blk((D_XATTN, n_mem)), blk((D_XATTN, n_mem)),
                   blk((D_XATTN, H_XATTN * n_mem)), blk((D_XATTN, H_XATTN * n_mem))],
        out_shape=[
            jax.ShapeDtypeStruct((depth, b, D_XATTN, n_mem), F32),
            jax.ShapeDtypeStruct((depth, b, D_XATTN, n_mem), F32),
            jax.ShapeDtypeStruct((depth, b, D_XATTN, H_XATTN * n_mem), BF16),
            jax.ShapeDtypeStruct((depth, b, D_XATTN, H_XATTN * n_mem), BF16),
        ],
        compiler_params=pltpu.CompilerParams(
            dimension_semantics=("arbitrary", "arbitrary"), vmem_limit_bytes=VMEM_LIMIT),
        name="mem_kv",
    )(mem, mem_norm.reshape(depth, 1, d), wmT)


def _merge_and_project(x, fo, go, xo, gate, bn_ref, wout_ref, gpost_ref):
    merged = jnp.concatenate([
        _rms(fo, bn_ref[:, 0:D_FOX]),
        _rms(go, bn_ref[:, D_FOX:D_FOX + D_GMLP]),
        _rms(xo, bn_ref[:, D_FOX + D_GMLP:]),
    ], axis=-1) * (gate * jax.nn.sigmoid(gate))
    y = _dot(merged.astype(BF16), wout_ref[...])
    return x + _rms(y, gpost_ref[...])


def _finish_prompt_body(x_ref, o_ref, rest_ref, mkbd_ref, mvbd_ref, wsp_ref, bsp_ref, bn_ref, wout_ref,
                        gpost_ref, y_ref):
    tm = x_ref.shape[1]
    rest = rest_ref[0]
    qx = rest[:, 0:D_XATTN]
    u = rest[:, D_XATTN:D_XATTN + D_GMLP]
    vg = rest[:, D_XATTN + D_GMLP:D_XATTN + 2 * D_GMLP]
    gate = rest[:, D_XATTN + 2 * D_GMLP:]

    wi = lax.broadcasted_iota(jnp.int32, wsp_ref.shape, 0) % CHUNK
    wj = lax.broadcasted_iota(jnp.int32, wsp_ref.shape, 1)
    w_tril = jnp.where(wj <= wi, wsp_ref[...], 0.0).astype(BF16)
    grp = lax.broadcasted_iota(jnp.int32, (CHUNK, D_GMLP), 1) // HEAD_DIM
    gos = []
    for c in range(tm // CHUNK):
        rs = slice(c * CHUNK, (c + 1) * CHUNK)
        mall = _dot(w_tril, vg[rs].astype(BF16))
        mixed = mall[0:CHUNK]
        for g in range(1, G_GMLP):
            mixed = jnp.where(grp == g, mall[g * CHUNK:(g + 1) * CHUNK], mixed)
        gos.append(u[rs] * (mixed + bsp_ref[...]))
    go = jnp.concatenate(gos, axis=0)

    s = _dot(qx.astype(BF16), mkbd_ref[0, 0]) * SCALE
    ps = []
    for hh in range(H_XATTN):
        sh = s[:, hh * N_MEM:(hh + 1) * N_MEM]
        e = jnp.exp(sh - jnp.max(sh, axis=-1, keepdims=True))
        ps.append(e / jnp.sum(e, axis=-1, keepdims=True))
    p = jnp.concatenate(ps, axis=-1).astype(BF16)
    xo = _dot(p, mvbd_ref[0, 0], _NT)

    y_ref[0] = _merge_and_project(x_ref[0], o_ref[0], go, xo, gate, bn_ref, wout_ref, gpost_ref)


def _finish_prompt(x, o, rest, mkbd, mvbd, layer, w, tm):
    b, s, d = x.shape
    full = lambda shape: pl.BlockSpec(shape, lambda bi, ti: (0,) * len(shape))
    tile = lambda width: pl.BlockSpec((1, tm, width), lambda bi, ti: (bi, ti, 0))
    kvspec = pl.BlockSpec((1, 1, D_XATTN, H_XATTN * N_MEM), lambda bi, ti: (layer, bi, 0, 0))
    return pl.pallas_call(
        _finish_prompt_body,
        grid=(b, s // tm),
        in_specs=[tile(d), tile(D_FOX), tile(D_REST), kvspec, kvspec,
                  full((G_GMLP * CHUNK, CHUNK)), full((CHUNK, D_GMLP)), full((1, D_MIX)),
                  full((D_MIX, d)), full((1, d))],
        out_specs=tile(d),
        out_shape=jax.ShapeDtypeStruct((b, s, d), F32),
        compiler_params=pltpu.CompilerParams(
            dimension_semantics=("arbitrary", "arbitrary"), vmem_limit_bytes=VMEM_LIMIT),
        name="finish_prompt",
    )(x, o, rest, mkbd, mvbd, w['wsp'], w['bsp'], w['bn'], w['wout'], w['gpost'])


def _proj_sample_body(x_ref, gpre_ref, ws_ref, wflT_ref, bf_ref, qg_ref, kgrow_ref, vgain_ref, gsum_ref,
                      q_ref, k_ref, v_ref, lfT_ref, rest_ref):
    h = _rms(x_ref[...], gpre_ref[...]).astype(BF16)
    z = _dot(h, ws_ref[...])

    def head_norm(t, gain):
        t2 = t * t
        t2h = t2.astype(BF16)
        t2l = (t2 - t2h.astype(F32)).astype(BF16)
        msq = (_dot(t2h, gsum_ref[...]) + _dot(t2l, gsum_ref[...])) * (1.0 / HEAD_DIM)
        return t * lax.rsqrt(msq + EPS) * gain

    q_ref[...] = head_norm(z[:, 0:D_FOX], qg_ref[...]).astype(BF16)
    k_ref[...] = head_norm(z[:, D_FOX:2 * D_FOX], kgrow_ref[...])
    v_ref[...] = z[:, 2 * D_FOX:3 * D_FOX]
    flT = _dot(wflT_ref[...], h, _NT)
    lfT_ref[...] = _log_sigmoid(flT[0:8] + bf_ref[...])
    base = 3 * D_FOX
    rest_ref[:, 0:D_XATTN + D_GMLP] = z[:, base:base + D_XATTN + D_GMLP]
    vg = z[:, base + D_XATTN + D_GMLP:base + D_XATTN + 2 * D_GMLP]
    rest_ref[:, D_XATTN + D_GMLP:D_XATTN + 2 * D_GMLP] = _rms(vg, vgain_ref[...])
    rest_ref[:, D_XATTN + 2 * D_GMLP:] = z[:, base + D_XATTN + 2 * D_GMLP:]


def _proj_sample(x, w, consts):
    n, d = x.shape
    full = lambda shape: pl.BlockSpec(shape, lambda i: (0,) * len(shape))
    return pl.pallas_call(
        _proj_sample_body,
        grid=(1,),
        in_specs=[full((n, d)), full((1, d)), full(w['ws'].shape), full(w['wflT'].shape), full((8, 1)),
                  full((1, D_FOX)), full((1, D_FOX)), full((1, D_GMLP)), full(consts['gsum'].shape)],
        out_specs=[full((n, D_FOX)), full((n, D_FOX)), full((n, D_FOX)), full((8, n)), full((n, D_REST))],
        out_shape=[
            jax.ShapeDtypeStruct((n, D_FOX), BF16),
            jax.ShapeDtypeStruct((n, D_FOX), F32),
            jax.ShapeDtypeStruct((n, D_FOX), F32),
            jax.ShapeDtypeStruct((8, n), F32),
            jax.ShapeDtypeStruct((n, D_REST), F32),
        ],
        compiler_params=pltpu.CompilerParams(
            dimension_semantics=("arbitrary",), vmem_limit_bytes=VMEM_LIMIT),
        name="proj_sample",
    )(x, w['gpre'], w['ws'], w['wflT'], w['bf'], w['qg'], w['kgrow'], w['vgain'], consts['gsum'])


def _expand_rows(x4, width):
    t = x4.shape[0]
    rep = jnp.concatenate([jnp.broadcast_to(x4[i:i + 1], (8, width)) for i in range(t)], axis=0)
    rowh = lax.broadcasted_iota(jnp.int32, (8 * t, width), 0) % 8
    colh = lax.broadcasted_iota(jnp.int32, (8 * t, width), 1) // HEAD_DIM
    return jnp.where(rowh == colh, rep, jnp.zeros_like(rep))


def _collapse_rows(x, t):
    width = x.shape[1]
    rowh = lax.broadcasted_iota(jnp.int32, x.shape, 0) % 8
    colh = lax.broadcasted_iota(jnp.int32, x.shape, 1) // HEAD_DIM
    xm = jnp.where(rowh == colh, x, 0.0)
    return jnp.concatenate([jnp.sum(xm[i * 8:(i + 1) * 8], axis=0, keepdims=True) for i in range(t)], axis=0)


def _decode_body(n_pages, t_new, pt_ref, q_ref, knew_ref, vnew_ref, lfT_ref, rest_ref, mkT_ref, mvT_ref,
                 tri_ref, *refs):
    k_refs = refs[0:n_pages]
    v_refs = refs[n_pages:2 * n_pages]
    lf_refs = refs[2 * n_pages:3 * n_pages]
    fo_ref, xo_ref = refs[3 * n_pages:]
    b = pl.program_id(0)
    nrow = 8 * t_new
    n_tok = lfT_ref.shape[1]

    qbd = _expand_rows(q_ref[...].astype(F32), D_FOX).astype(BF16)

    tri = tri_ref[...]
    carry = jnp.zeros((8, 1), F32)
    cks = []
    for p in range(n_pages):
        c = _cumsum_lanes(lf_refs[p][...], tri) + carry
        carry = c[:, PAGE_SIZE - 1:PAGE_SIZE]
        cks.append(c)
    jj = lax.broadcasted_iota(jnp.int32, (n_tok, LANES), 0)
    tt = lax.broadcasted_iota(jnp.int32, (n_tok, LANES), 1)
    sel = ((jj >= b * t_new) & (jj <= b * t_new + tt) & (tt < t_new)).astype(BF16)
    cn = _cumsum_lanes(lfT_ref[...], sel) + carry
    cq = jnp.concatenate([cn[:, i:i + 1] for i in range(t_new)], axis=0)

    m = jnp.full((nrow, 1), NEG, F32)
    l = jnp.zeros((nrow, 1), F32)
    acc = jnp.zeros((nrow, D_FOX), F32)
    for p in range(n_pages):
        s = _dot(qbd, k_refs[p][...].astype(BF16))
        s = s + (cq - jnp.concatenate([cks[p]] * t_new, axis=0))
        m_new = jnp.maximum(m, jnp.max(s, axis=-1, keepdims=True))
        alpha = jnp.exp(m - m_new)
        e = jnp.exp(s - m_new)
        l = alpha * l + jnp.sum(e, axis=-1, keepdims=True)
        acc = alpha * acc + _dot(e.astype(BF16), v_refs[p][...].astype(BF16), _NT)
        m = m_new

    pad = jnp.zeros((16 - t_new, D_FOX), F32)
    knew = jnp.concatenate([knew_ref[...], pad], axis=0).astype(BF16)
    vnew = jnp.concatenate([vnew_ref[...], pad], axis=0).astype(BF16)
    s = _dot(qbd, knew, _NT)
    s = s + (cq - jnp.concatenate([cn[:, 0:16]] * t_new, axis=0))
    qt = lax.broadcasted_iota(jnp.int32, (nrow, 16), 0) // 8
    kt = lax.broadcasted_iota(jnp.int32, (nrow, 16), 1)
    s = jnp.where(kt <= qt, s, NEG)
    m_new = jnp.maximum(m, jnp.max(s, axis=-1, keepdims=True))
    alpha = jnp.exp(m - m_new)
    e = jnp.exp(s - m_new)
    l = alpha * l + jnp.sum(e, axis=-1, keepdims=True)
    acc = alpha * acc + _dot(e.astype(BF16), vnew)
    fo_ref[...] = _collapse_rows(acc / l, t_new)

    qx = rest_ref[:, 0:D_XATTN]
    qxbd = _expand_rows(qx, D_XATTN).astype(BF16)
    sx = _dot(qxbd, mkT_ref[...].astype(BF16)) * SCALE
    ex = jnp.exp(sx - jnp.max(sx, axis=-1, keepdims=True))
    px = (ex / jnp.sum(ex, axis=-1, keepdims=True)).astype(BF16)
    xo_ref[...] = _collapse_rows(_dot(px, mvT_ref[...].astype(BF16), _NT), t_new)


def _decode(layer, page_table, q, knew, vnew, lfT, rest, kc, vc, lfc, mkc, mvc, tri_page):
    nb, t_new, _ = q.shape
    n_pages = page_table.shape[1]
    row = lambda width: pl.BlockSpec((None, t_new, width), lambda bi, pt: (bi, 0, 0))
    page = lambda rows, p: pl.BlockSpec((None, None, rows, PAGE_SIZE), lambda bi, pt: (layer, pt[bi, p], 0, 0))
    mem = pl.BlockSpec((None, None, D_XATTN, N_MEM), lambda bi, pt: (layer, bi, 0, 0))
    in_specs = [row(D_FOX), row(D_FOX), row(D_FOX),
                pl.BlockSpec(lfT.shape, lambda bi, pt: (0, 0)),
                row(D_REST), mem, mem,
                pl.BlockSpec(tri_page.shape, lambda bi, pt: (0, 0))]
    in_specs += [page(D_FOX, p) for p in range(n_pages)]
    in_specs += [page(D_FOX, p) for p in range(n_pages)]
    in_specs += [page(H_FOX, p) for p in range(n_pages)]
    return pl.pallas_call(
        functools.partial(_decode_body, n_pages, t_new),
        grid_spec=pltpu.PrefetchScalarGridSpec(
            num_scalar_prefetch=1, grid=(nb,), in_specs=in_specs,
            out_specs=[row(D_FOX), row(D_XATTN)]),
        out_shape=[jax.ShapeDtypeStruct((nb, t_new, D_FOX), F32),
                   jax.ShapeDtypeStruct((nb, t_new, D_XATTN), F32)],
        compiler_params=pltpu.CompilerParams(
            dimension_semantics=("arbitrary",), vmem_limit_bytes=VMEM_LIMIT),
        name="decode",
    )(page_table, q, knew, vnew, lfT, rest, mkc, mvc, tri_page, *([kc] * n_pages), *([vc] * n_pages),
      *([lfc] * n_pages))


def _finish_sample_body(t_new, wsp_ref, bsp_ref, x_ref, fo_ref, xo_ref, rest_ref, bn_ref, wout_ref,
                        gpost_ref, y_ref):
    n = x_ref.shape[0]
    u = rest_ref[:, D_XATTN:D_XATTN + D_GMLP]
    vg = rest_ref[:, D_XATTN + D_GMLP:D_XATTN + 2 * D_GMLP]
    gate = rest_ref[:, D_XATTN + 2 * D_GMLP:]

    pos = lax.broadcasted_iota(jnp.int32, (n, D_GMLP), 0) % t_new
    grp = lax.broadcasted_iota(jnp.int32, (n, D_GMLP), 1) // HEAD_DIM
    mixed = jnp.zeros((n, D_GMLP), F32)
    for g in range(G_GMLP):
        for i in range(t_new):
            mixed = jnp.where((grp == g) & (pos == i), bsp_ref[g, i], mixed)
    for k in range(t_new):
        coef = jnp.zeros((n, D_GMLP), F32)
        for g in range(G_GMLP):
            for i in range(k, t_new):
                coef = jnp.where((grp == g) & (pos == i), wsp_ref[g, i * t_new + i - k], coef)
        shifted = vg if k == 0 else pltpu.roll(vg, k, 0)
        mixed = mixed + coef * shifted
    go = u * mixed
    y_ref[...] = _merge_and_project(x_ref[...], fo_ref[...], go, xo_ref[...], gate, bn_ref, wout_ref,
                                    gpost_ref)


def _finish_sample(x, fo, xo, rest, w_small, b_small, w, t_new):
    n, d = x.shape
    full = lambda shape: pl.BlockSpec(shape, lambda i: (0,) * len(shape))
    smem = pl.BlockSpec(memory_space=pltpu.SMEM)
    return pl.pallas_call(
        functools.partial(_finish_sample_body, t_new),
        grid=(1,),
        in_specs=[smem, smem, full((n, d)), full((n, D_FOX)), full((n, D_XATTN)), full((n, D_REST)),
                  full((1, D_MIX)), full((D_MIX, d)), full((1, d))],
        out_specs=full((n, d)),
        out_shape=jax.ShapeDtypeStruct((n, d), F32),
        compiler_params=pltpu.CompilerParams(
            dimension_semantics=("arbitrary",), vmem_limit_bytes=VMEM_LIMIT),
        name="finish_sample",
    )(w_small, b_small, x, fo, xo, rest, w['bn'], w['wout'], w['gpost'])


def _constants(tm):
    f = jnp.arange(D_FOX)
    gsum = (f[:, None] // HEAD_DIM == f[None, :] // HEAD_DIM).astype(BF16)
    a = jnp.arange(H_FOX * AUG)
    place = ((a[None, :] // AUG == f[:, None] // HEAD_DIM)
             & (a[None, :] % AUG == f[:, None] % HEAD_DIM)).astype(BF16)
    r = jnp.arange(32)
    ah, ao = a // AUG, a % AUG - HEAD_DIM
    selq = jnp.where((ao[None, :] >= 0) & (ao[None, :] < 3) & (r[:, None] == ao[None, :] * 8 + ah[None, :]), 1.0,
                     jnp.where((ao[None, :] >= 3) & (ao[None, :] < 6) & (r[:, None] == 24), 1.0, 0.0)).astype(BF16)
    fh, fo = f // HEAD_DIM, f % HEAD_DIM
    selk = jnp.where((fo[:, None] < 3) & (r[None, :] == 24), 1.0,
                     jnp.where((fo[:, None] >= 3) & (fo[:, None] < 6)
                               & (r[None, :] == (fo[:, None] - 3) * 8 + fh[:, None]), -1.0, 0.0)).astype(BF16)
    t = jnp.arange(tm)
    tri = (t[:, None] <= t[None, :]).astype(BF16)
    p = jnp.arange(PAGE_SIZE)
    tri_page = (p[:, None] <= p[None, :]).astype(BF16)
    return dict(gsum=gsum, place=place, selq=selq, selk=selk, tri=tri, tri_page=tri_page)


def _layer_weights(l, norm_pre, w_in, b_forget, q_norm, k_norm, gmlp_v_norm, w_spatial, b_spatial,
                   branch_norm, w_out, norm_post):
    d = w_in.shape[1]
    wl = w_in[l]
    q_cols = wl[:, 0:D_FOX]
    kv_cols = wl[:, D_FOX:3 * D_FOX]
    rest_cols = wl[:, 3 * D_FOX:3 * D_FOX + D_REST]
    fl_cols = wl[:, 3 * D_FOX + D_REST:]
    wflT = jnp.concatenate([fl_cols.T, jnp.zeros((8, d), F32)], axis=0).astype(BF16)
    return dict(
        gpre=norm_pre[l].reshape(1, d),
        wa=jnp.concatenate([q_cols, rest_cols], axis=1).astype(BF16),
        wkvT=kv_cols.T.astype(BF16),
        ws=wl[:, 0:3 * D_FOX + D_REST].astype(BF16),
        wflT=wflT,
        bf=b_forget[l].reshape(H_FOX, 1),
        qg=(jnp.tile(q_norm[l], H_FOX) * SCALE).reshape(1, D_FOX),
        kg=k_norm[l].reshape(HEAD_DIM, 1),
        kgrow=jnp.tile(k_norm[l], H_FOX).reshape(1, D_FOX),
        vgain=gmlp_v_norm[l].reshape(1, D_GMLP),
        wsp=w_spatial[l].reshape(G_GMLP * CHUNK, CHUNK),
        bsp=jnp.repeat(b_spatial[l].T, HEAD_DIM, axis=1),
        bn=branch_norm[l].reshape(1, D_MIX),
        wout=w_out[l].astype(BF16),
        gpost=norm_post[l].reshape(1, d),
    )


def kernel(x_prompt, x_sample, mem_prompt, cache_fox_k, cache_fox_v, cache_fox_lf, cache_mem_k, cache_mem_v,
           page_table, norm_pre, w_in, b_forget, q_norm, k_norm, gmlp_v_norm, w_spatial, b_spatial, mem_norm,
           w_mem_kv, branch_norm, w_out, norm_post, *, tm=256, bq=512):
    depth = w_in.shape[0]
    b, s, d = x_prompt.shape
    nb, t_new, _ = x_sample.shape
    n_pool = cache_fox_k.shape[1]
    tm = min(tm, s)
    bq = min(bq, s)
    consts = _constants(tm)

    kc = jnp.transpose(cache_fox_k, (0, 1, 3, 4, 2)).reshape(depth, n_pool, D_FOX, PAGE_SIZE)
    vc = jnp.transpose(cache_fox_v, (0, 1, 3, 4, 2)).reshape(depth, n_pool, D_FOX, PAGE_SIZE)
    lfc = jnp.transpose(cache_fox_lf, (0, 1, 3, 2))
    mkc = jnp.transpose(cache_mem_k, (0, 1, 3, 4, 2)).reshape(depth, nb, D_XATTN, N_MEM)
    mvc = jnp.transpose(cache_mem_v, (0, 1, 3, 4, 2)).reshape(depth, nb, D_XATTN, N_MEM)

    wmT = jnp.transpose(w_mem_kv, (0, 2, 1)).astype(BF16)
    mkT, mvT, mkbd, mvbd = _mem_kv(mem_prompt, mem_norm, wmT)

    xp = x_prompt
    xs = x_sample.reshape(nb * t_new, d)
    kTs, vTs, lfTs, kss, vss, lfss, vgs = [], [], [], [], [], [], []
    for l in range(depth):
        w = _layer_weights(l, norm_pre, w_in, b_forget, q_norm, k_norm, gmlp_v_norm, w_spatial, b_spatial,
                           branch_norm, w_out, norm_post)
        qa, kTa, vTa, kT, vT, lfT, rest = _proj_prompt(xp, w, consts, tm)
        o = _fox_prompt(qa, kTa, vTa, bq, bq)
        xp = _finish_prompt(xp, o, rest, mkbd, mvbd, l, w, tm)
        kTs.append(kT); vTs.append(vT); lfTs.append(lfT)

        q_s, k_s, v_s, lfT_s, rest_s = _proj_sample(xs, w, consts)
        r3 = lambda a: a.reshape(nb, t_new, a.shape[-1])
        fo, xo = _decode(l, page_table, r3(q_s), r3(k_s), r3(v_s), lfT_s, r3(rest_s), kc, vc, lfc, mkc, mvc,
                         consts['tri_page'])
        w_small = w_spatial[l][:, :t_new, :t_new].reshape(G_GMLP, t_new * t_new)
        b_small = b_spatial[l][:, :t_new]
        xs = _finish_sample(xs, fo.reshape(nb * t_new, D_FOX), xo.reshape(nb * t_new, D_XATTN), rest_s,
                            w_small, b_small, w, t_new)
        kss.append(k_s); vss.append(v_s); lfss.append(lfT_s)
        vgs.append(rest_s[:, D_XATTN + D_GMLP:D_XATTN + 2 * D_GMLP])

    def heads_last(ts):
        return jnp.transpose(jnp.stack(ts), (0, 1, 4, 2, 3))

    mem_view = lambda m: jnp.transpose(m.reshape(depth, b, H_XATTN, HEAD_DIM, N_MEM), (0, 1, 4, 2, 3))
    return (
        xp,
        xs.reshape(nb, t_new, d),
        heads_last(kTs),
        heads_last(vTs),
        jnp.transpose(jnp.stack(lfTs), (0, 1, 3, 2)),
        mem_view(mkT),
        mem_view(mvT),
        jnp.stack(kss).reshape(depth, nb, t_new, H_FOX, HEAD_DIM),
        jnp.stack(vss).reshape(depth, nb, t_new, H_FOX, HEAD_DIM),
        jnp.transpose(jnp.stack(lfss), (0, 2, 1)).reshape(depth, nb, t_new, H_FOX),
        jnp.stack(vgs).reshape(depth, nb, t_new, D_GMLP),
    )
```

```python
import functools

import jax
import jax.numpy as jnp
from jax import lax
from jax.experimental import pallas as pl
from jax.experimental.pallas import tpu as pltpu

HEAD_DIM = 64
H_FOX = 8
D_FOX = H_FOX * HEAD_DIM
G_GMLP = 4
D_GMLP = G_GMLP * HEAD_DIM
H_XATTN = 4
D_XATTN = H_XATTN * HEAD_DIM
D_MIX = D_FOX + D_GMLP + D_XATTN
N_MEM = 256
CHUNK = 128
PAGE_SIZE = 128
EPS = 1e-6
NEG = -1e30
SCALE = HEAD_DIM ** -0.5
LANES = 128
AUG = 2 * HEAD_DIM
D_REST = D_XATTN + 2 * D_GMLP + D_MIX
VMEM_LIMIT = 56 * 1024 * 1024
TM_PROJ = 512
TM_FINISH = 256
BQ = 512

F32 = jnp.float32
BF16 = jnp.bfloat16

_NN = (((1,), (0,)), ((), ()))
_NT = (((1,), (1,)), ((), ()))
_TN = (((0,), (0,)), ((), ()))


def _dot(a, b, dims=_NN):
    return lax.dot_general(a, b, dims, preferred_element_type=F32)


def _split3(x):
    hi = x.astype(BF16).astype(F32)
    r = x - hi
    mid = r.astype(BF16).astype(F32)
    lo = (r - mid).astype(BF16).astype(F32)
    return hi, mid, lo


def _log_sigmoid(x):
    return jnp.minimum(x, 0.0) - jnp.log1p(jnp.exp(-jnp.abs(x)))


def _rms(x, gain):
    return x * lax.rsqrt(jnp.mean(x * x, axis=-1, keepdims=True) + EPS) * gain


def _cumsum_lanes(xT, tri):
    hi, mid, lo = _split3(xT)
    parts = jnp.concatenate([hi, mid, lo, jnp.zeros_like(hi)], axis=0).astype(BF16)
    cs = _dot(parts, tri)
    return cs[0:8] + cs[8:16] + cs[16:24]


def _proj_prompt_body(x_ref, gpre_ref, wa_ref, wqkvT_ref, wflT_ref, bf_ref, qg_ref, kg_ref, vgain_ref,
                      selq_ref, selk_ref, tri_ref,
                      qTa_ref, ka_ref, vTa_ref, kT_ref, vT_ref, lfT_ref, rest_ref, carry_ref):
    tm = x_ref.shape[1]

    @pl.when(pl.program_id(1) == 0)
    def _():
        carry_ref[...] = jnp.zeros_like(carry_ref)

    x = x_ref[0]
    h = _rms(x, gpre_ref[...]).astype(BF16)

    za = _dot(h, wa_ref[...])
    rest_ref[0, :, 0:D_XATTN + D_GMLP] = za[:, 0:D_XATTN + D_GMLP]
    vg = za[:, D_XATTN + D_GMLP:D_XATTN + 2 * D_GMLP]
    rest_ref[0, :, D_XATTN + D_GMLP:D_XATTN + 2 * D_GMLP] = _rms(vg, vgain_ref[...])
    rest_ref[0, :, D_XATTN + 2 * D_GMLP:] = za[:, D_XATTN + 2 * D_GMLP:]

    zT = _dot(wqkvT_ref[...], h, _NT)
    flT = _dot(wflT_ref[...], h, _NT)
    lfT = _log_sigmoid(flT[0:8] + bf_ref[...])
    lfT_ref[0] = lfT

    cT = _cumsum_lanes(lfT, tri_ref[...]) + carry_ref[:, 0:1]
    carry_ref[...] = jnp.broadcast_to(cT[:, tm - 1:tm], carry_ref.shape)
    chi, cmid, clo = _split3(cT)
    cparts = jnp.concatenate([chi, cmid, clo, jnp.ones_like(chi)], axis=0).astype(BF16)

    qaugT = _dot(selq_ref[...], cparts)
    kaugT = _dot(selk_ref[...], cparts)
    row = lax.broadcasted_iota(jnp.int32, (HEAD_DIM, tm), 0)
    vaugT = jnp.where(row == 0, 1.0, 0.0).astype(F32)

    def head_norm(t, gain):
        return t * lax.rsqrt(jnp.mean(t * t, axis=0, keepdims=True) + EPS) * gain

    for hh in range(H_FOX):
        sl = slice(hh * HEAD_DIM, (hh + 1) * HEAD_DIM)
        qn = head_norm(zT[sl], qg_ref[...])
        qTa_ref[0, hh] = jnp.concatenate([qn, qaugT[sl]], axis=0).astype(BF16)
        kn = head_norm(zT[D_FOX + hh * HEAD_DIM:D_FOX + (hh + 1) * HEAD_DIM], kg_ref[...])
        kT_ref[0, hh] = kn
        ka_ref[0, hh] = jnp.concatenate([kn, kaugT[sl]], axis=0).T.astype(BF16)
        vh = zT[2 * D_FOX + hh * HEAD_DIM:2 * D_FOX + (hh + 1) * HEAD_DIM]
        vT_ref[0, hh] = vh
        vTa_ref[0, hh] = jnp.concatenate([vh, vaugT], axis=0).astype(BF16)


def _proj_prompt(x, w, consts, tm):
    b, s, d = x.shape
    n_t = s // tm
    full = lambda shape: pl.BlockSpec(shape, lambda bi, ti: (0,) * len(shape), pipeline_mode=pl.Buffered(1))
    in_specs = [
        pl.BlockSpec((1, tm, d), lambda bi, ti: (bi, ti, 0)),
        full((1, d)), full(w['wa'].shape), full(w['wqkvT'].shape), full(w['wflT'].shape),
        full((8, 1)), full((HEAD_DIM, 1)), full((HEAD_DIM, 1)), full((1, D_GMLP)),
        full(consts['selq'].shape), full(consts['selk'].shape), full(consts['tri'].shape),
    ]
    out_shape = [
        jax.ShapeDtypeStruct((b, H_FOX, AUG, s), BF16),
        jax.ShapeDtypeStruct((b, H_FOX, s, AUG), BF16),
        jax.ShapeDtypeStruct((b, H_FOX, AUG, s), BF16),
        jax.ShapeDtypeStruct((b, H_FOX, HEAD_DIM, s), F32),
        jax.ShapeDtypeStruct((b, H_FOX, HEAD_DIM, s), F32),
        jax.ShapeDtypeStruct((b, H_FOX, s), F32),
        jax.ShapeDtypeStruct((b, s, D_REST), F32),
    ]
    out_specs = [
        pl.BlockSpec((1, H_FOX, AUG, tm), lambda bi, ti: (bi, 0, 0, ti)),
        pl.BlockSpec((1, H_FOX, tm, AUG), lambda bi, ti: (bi, 0, ti, 0)),
        pl.BlockSpec((1, H_FOX, AUG, tm), lambda bi, ti: (bi, 0, 0, ti)),
        pl.BlockSpec((1, H_FOX, HEAD_DIM, tm), lambda bi, ti: (bi, 0, 0, ti)),
        pl.BlockSpec((1, H_FOX, HEAD_DIM, tm), lambda bi, ti: (bi, 0, 0, ti)),
        pl.BlockSpec((1, H_FOX, tm), lambda bi, ti: (bi, 0, ti)),
        pl.BlockSpec((1, tm, D_REST), lambda bi, ti: (bi, ti, 0)),
    ]
    return pl.pallas_call(
        _proj_prompt_body,
        grid=(b, n_t),
        in_specs=in_specs, out_specs=out_specs, out_shape=out_shape,
        scratch_shapes=[pltpu.VMEM((8, LANES), F32)],
        compiler_params=pltpu.CompilerParams(
            dimension_semantics=("arbitrary", "arbitrary"), vmem_limit_bytes=VMEM_LIMIT),
        name="proj_prompt",
    )(x, w['gpre'], w['wa'], w['wqkvT'], w['wflT'], w['bf'], w['qg'], w['kg'], w['vgain'],
      consts['selq'], consts['selk'], consts['tri'])


def _fox_prompt_body(qTa_ref, ka_ref, vTa_ref, o_ref, m_ref, acc_ref):
    bq = qTa_ref.shape[3]
    bk = ka_ref.shape[2]
    i = pl.program_id(1)
    j = pl.program_id(2)

    @pl.when(j == 0)
    def _():
        m_ref[...] = jnp.full_like(m_ref, NEG)
        acc_ref[...] = jnp.zeros_like(acc_ref)

    def all_heads(diagonal):
        if diagonal:
            keep = (lax.broadcasted_iota(jnp.int32, (bk, bq), 0)
                    <= lax.broadcasted_iota(jnp.int32, (bk, bq), 1))
        sT_next = _dot(ka_ref[0, 0], qTa_ref[0, 0])
        for hh in range(H_FOX):
            sT = sT_next
            if diagonal:
                sT = jnp.where(keep, sT, NEG)
            m_prev = m_ref[hh][0:1]
            m_new = jnp.maximum(m_prev, jnp.max(sT, axis=0, keepdims=True))
            alpha = jnp.exp(m_prev - m_new)
            pT = jnp.exp(sT - m_new).astype(BF16)
            if hh + 1 < H_FOX:
                sT_next = _dot(ka_ref[0, hh + 1], qTa_ref[0, hh + 1])
            pv = _dot(vTa_ref[0, hh], pT)
            acc_ref[hh] = alpha * acc_ref[hh] + pv
            m_ref[hh] = jnp.broadcast_to(m_new, (8, bq))

    @pl.when(j < i)
    def _():
        all_heads(False)

    @pl.when(j == i)
    def _():
        all_heads(True)
        lane = lax.broadcasted_iota(jnp.int32, (bq, LANES), 1)
        for pair in range(H_FOX // 2):
            a0 = acc_ref[2 * pair]
            a1 = acc_ref[2 * pair + 1]
            o0 = (a0 * (1.0 / a0[HEAD_DIM:HEAD_DIM + 1])).T
            o1 = (a1 * (1.0 / a1[HEAD_DIM:HEAD_DIM + 1])).T
            o_ref[0, :, pair * LANES:(pair + 1) * LANES] = jnp.where(
                lane < HEAD_DIM, o0, pltpu.roll(o1, HEAD_DIM, 1))


def _fox_prompt(qTa, ka, vTa, bq, bk):
    b, _, _, s = qTa.shape
    assert bq == bk
    nq, nk = s // bq, s // bk
    return pl.pallas_call(
        _fox_prompt_body,
        grid=(b, nq, nk),
        in_specs=[
            pl.BlockSpec((1, H_FOX, AUG, bq), lambda bi, i, j: (bi, 0, 0, i)),
            pl.BlockSpec((1, H_FOX, bk, AUG), lambda bi, i, j: (bi, 0, jnp.minimum(i, j), 0)),
            pl.BlockSpec((1, H_FOX, AUG, bk), lambda bi, i, j: (bi, 0, 0, jnp.minimum(i, j))),
        ],
        out_specs=pl.BlockSpec((1, bq, D_FOX), lambda bi, i, j: (bi, i, 0)),
        out_shape=jax.ShapeDtypeStruct((b, s, D_FOX), F32),
        scratch_shapes=[pltpu.VMEM((H_FOX, 8, bq), F32), pltpu.VMEM((H_FOX, AUG, bq), F32)],
        compiler_params=pltpu.CompilerParams(
            dimension_semantics=("arbitrary", "arbitrary", "arbitrary"), vmem_limit_bytes=VMEM_LIMIT),
        name="fox_prompt",
    )(qTa, ka, vTa)


def _mem_kv_body(mem_ref, gain_ref, wT_ref, mkT_ref, mvT_ref, mkbd_ref, mvbd_ref):
    hm = _rms(mem_ref[0], gain_ref[0]).astype(BF16)
    kvT = _dot(wT_ref[0], hm, _NT)
    mkT = kvT[:D_XATTN]
    mvT = kvT[D_XATTN:]
    mkT_ref[0, 0] = mkT
    mvT_ref[0, 0] = mvT
    mkbd_ref[...] = jnp.zeros_like(mkbd_ref)
    mvbd_ref[...] = jnp.zeros_like(mvbd_ref)
    for hh in range(H_XATTN):
        r = slice(hh * HEAD_DIM, (hh + 1) * HEAD_DIM)
        c = slice(hh * N_MEM, (hh + 1) * N_MEM)
        mkbd_ref[0, 0, r, c] = mkT[r].astype(BF16)
        mvbd_ref[0, 0, r, c] = mvT[r].astype(BF16)


def _mem_kv(mem, mem_norm, wmT):
    b, n_mem, d = mem.shape
    depth = wmT.shape[0]
    blk = lambda shape: pl.BlockSpec((1, 1) + shape, lambda l, bi: (l, bi, 0, 0))
    return pl.pallas_call(
        _mem_kv_body,
        grid=(depth, b),
        in_specs=[
            pl.BlockSpec((1, n_mem, d), lambda l, bi: (bi, 0, 0)),
            pl.BlockSpec((1, 1, d), lambda l, bi: (l, 0, 0)),
            pl.BlockSpec((1, 2 * D_XATTN, d), lambda l, bi: (l, 0, 0)),
        ],
        out_specs=[blk((D_XATTN, n_mem)), blk((D_XATTN, n_mem)),
                   blk((D_XATTN, H_XATTN * n_mem)), blk((D_XATTN, H_XATTN * n_mem))],
        out_shape=[
            jax.ShapeDtypeStruct((depth, b, D_XATTN, n_mem), F32),
            jax.ShapeDtypeStruct((depth, b, D_XATTN, n_mem), F32),
            jax.ShapeDtypeStruct((depth, b, D_XATTN, H_XATTN * n_mem), BF16),
            jax.ShapeDtypeStruct((depth, b, D_XATTN, H_XATTN * n_mem), BF16),
        ],
        compiler_params=pltpu.CompilerParams(
            dimension_semantics=("arbitrary", "arbitrary"), vmem_limit_bytes=VMEM_LIMIT),
        name="mem_kv",
    )(mem, mem_norm.reshape(depth, 1, d), wmT)


def _merge_and_project(x, fo, go, xo, gate, bn_ref, wout_ref, gpost_ref):
    merged = jnp.concatenate([
        _rms(fo, bn_ref[:, 0:D_FOX]),
        _rms(go, bn_ref[:, D_FOX:D_FOX + D_GMLP]),
        _rms(xo, bn_ref[:, D_FOX + D_GMLP:]),
    ], axis=-1) * (gate * jax.nn.sigmoid(gate))
    y = _dot(merged.astype(BF16), wout_ref[...])
    return x + _rms(y, gpost_ref[...])


def _finish_prompt_body(x_ref, o_ref, rest_ref, mkbd_ref, mvbd_ref, wsp_ref, bsp_ref, bn_ref, wout_ref,
                        gpost_ref, y_ref):
    tm = x_ref.shape[1]
    rest = rest_ref[0]
    qx = rest[:, 0:D_XATTN]
    u = rest[:, D_XATTN:D_XATTN + D_GMLP]
    vg = rest[:, D_XATTN + D_GMLP:D_XATTN + 2 * D_GMLP]
    gate = rest[:, D_XATTN + 2 * D_GMLP:]

    wi = lax.broadcasted_iota(jnp.int32, wsp_ref.shape, 0) % CHUNK
    wj = lax.broadcasted_iota(jnp.int32, wsp_ref.shape, 1)
    w_tril = jnp.where(wj <= wi, wsp_ref[...], 0.0).astype(BF16)
    grp = lax.broadcasted_iota(jnp.int32, (CHUNK, D_GMLP), 1) // HEAD_DIM
    gos = []
    for c in range(tm // CHUNK):
        rs = slice(c * CHUNK, (c + 1) * CHUNK)
        mall = _dot(w_tril, vg[rs].astype(BF16))
        mixed = mall[0:CHUNK]
        for g in range(1, G_GMLP):
            mixed = jnp.where(grp == g, mall[g * CHUNK:(g + 1) * CHUNK], mixed)
        gos.append(u[rs] * (mixed + bsp_ref[...]))
    go = jnp.concatenate(gos, axis=0)

    s = _dot(qx.astype(BF16), mkbd_ref[0, 0]) * SCALE
    ps = []
    for hh in range(H_XATTN):
        sh = s[:, hh * N_MEM:(hh + 1) * N_MEM]
        e = jnp.exp(sh - jnp.max(sh, axis=-1, keepdims=True))
        ps.append(e / jnp.sum(e, axis=-1, keepdims=True))
    p = jnp.concatenate(ps, axis=-1).astype(BF16)
    xo = _dot(p, mvbd_ref[0, 0], _NT)

    y_ref[0] = _merge_and_project(x_ref[0], o_ref[0], go, xo, gate, bn_ref, wout_ref, gpost_ref)


def _finish_prompt(x, o, rest, mkbd, mvbd, layer, w, tm):
    b, s, d = x.shape
    full = lambda shape: pl.BlockSpec(shape, lambda bi, ti: (0,) * len(shape))
    tile = lambda width: pl.BlockSpec((1, tm, width), lambda bi, ti: (bi, ti, 0))
    kvspec = pl.BlockSpec((1, 1, D_XATTN, H_XATTN * N_MEM), lambda bi, ti: (layer, bi, 0, 0))
    return pl.pallas_call(
        _finish_prompt_body,
        grid=(b, s // tm),
        in_specs=[tile(d), tile(D_FOX), tile(D_REST), kvspec, kvspec,
                  full((G_GMLP * CHUNK, CHUNK)), full((CHUNK, D_GMLP)), full((1, D_MIX)),
                  full((D_MIX, d)), full((1, d))],
        out_specs=tile(d),
        out_shape=jax.ShapeDtypeStruct((b, s, d), F32),
        compiler_params=pltpu.CompilerParams(
            dimension_semantics=("arbitrary", "arbitrary"), vmem_limit_bytes=VMEM_LIMIT),
        name="finish_prompt",
    )(x, o, rest, mkbd, mvbd, w['wsp'], w['bsp'], w['bn'], w['wout'], w['gpost'])


def _proj_sample_body(x_ref, gpre_ref, ws_ref, wflT_ref, bf_ref, qg_ref, kgrow_ref, vgain_ref, gsum_ref,
                      q_ref, k_ref, v_ref, lfT_ref, rest_ref):
    h = _rms(x_ref[...], gpre_ref[...]).astype(BF16)
    z = _dot(h, ws_ref[...])

    def head_norm(t, gain):
        t2 = t * t
        t2h = t2.astype(BF16)
        t2l = (t2 - t2h.astype(F32)).astype(BF16)
        msq = (_dot(t2h, gsum_ref[...]) + _dot(t2l, gsum_ref[...])) * (1.0 / HEAD_DIM)
        return t * lax.rsqrt(msq + EPS) * gain

    q_ref[...] = head_norm(z[:, 0:D_FOX], qg_ref[...]).astype(BF16)
    k_ref[...] = head_norm(z[:, D_FOX:2 * D_FOX], kgrow_ref[...])
    v_ref[...] = z[:, 2 * D_FOX:3 * D_FOX]
    flT = _dot(wflT_ref[...], h, _NT)
    lfT_ref[...] = _log_sigmoid(flT[0:8] + bf_ref[...])
    base = 3 * D_FOX
    rest_ref[:, 0:D_XATTN + D_GMLP] = z[:, base:base + D_XATTN + D_GMLP]
    vg = z[:, base + D_XATTN + D_GMLP:base + D_XATTN + 2 * D_GMLP]
    rest_ref[:, D_XATTN + D_GMLP:D_XATTN + 2 * D_GMLP] = _rms(vg, vgain_ref[...])
    rest_ref[:, D_XATTN + 2 * D_GMLP:] = z[:, base + D_XATTN + 2 * D_GMLP:]


def _proj_sample(x, w, consts):
    n, d = x.shape
    full = lambda shape: pl.BlockSpec(shape, lambda i: (0,) * len(shape))
    return pl.pallas_call(
        _proj_sample_body,
        grid=(1,),
        in_specs=[full((n, d)), full((1, d)), full(w['ws'].shape), full(w['wflT'].shape), full((8, 1)),
                  full((1, D_FOX)), full((1, D_FOX)), full((1, D_GMLP)), full(consts['gsum'].shape)],
        out_specs=[full((n, D_FOX)), full((n, D_FOX)), full((n, D_FOX)), full((8, n)), full((n, D_REST))],
        out_shape=[
            jax.ShapeDtypeStruct((n, D_FOX), BF16),
            jax.ShapeDtypeStruct((n, D_FOX), F32),
            jax.ShapeDtypeStruct((n, D_FOX), F32),
            jax.ShapeDtypeStruct((8, n), F32),
            jax.ShapeDtypeStruct((n, D_REST), F32),
        ],
        compiler_params=pltpu.CompilerParams(
            dimension_semantics=("arbitrary",), vmem_limit_bytes=VMEM_LIMIT),
        name="proj_sample",
    )(x, w['gpre'], w['ws'], w['wflT'], w['bf'], w['qgrow'], w['kgrow'], w['vgain'], consts['gsum'])


def _expand_rows(x4, width):
    t = x4.shape[0]
    rep = jnp.concatenate([jnp.broadcast_to(x4[i:i + 1], (8, width)) for i in range(t)], axis=0)
    rowh = lax.broadcasted_iota(jnp.int32, (8 * t, width), 0) % 8
    colh = lax.broadcasted_iota(jnp.int32, (8 * t, width), 1) // HEAD_DIM
    return jnp.where(rowh == colh, rep, jnp.zeros_like(rep))


def _collapse_rows(x, t):
    width = x.shape[1]
    rowh = lax.broadcasted_iota(jnp.int32, x.shape, 0) % 8
    colh = lax.broadcasted_iota(jnp.int32, x.shape, 1) // HEAD_DIM
    xm = jnp.where(rowh == colh, x, 0.0)
    return jnp.concatenate([jnp.sum(xm[i * 8:(i + 1) * 8], axis=0, keepdims=True) for i in range(t)], axis=0)


def _decode_body(n_pages, t_new, pt_ref, q_ref, knew_ref, vnew_ref, lfT_ref, rest_ref, mkT_ref, mvT_ref,
                 tri_ref, *refs):
    k_refs = refs[0:n_pages]
    v_refs = refs[n_pages:2 * n_pages]
    lf_refs = refs[2 * n_pages:3 * n_pages]
    fo_ref, xo_ref = refs[3 * n_pages:]
    b = pl.program_id(0)
    nrow = 8 * t_new
    n_tok = lfT_ref.shape[1]

    qbd = _expand_rows(q_ref[...].astype(F32), D_FOX).astype(BF16)

    lf_all = jnp.concatenate([lf_refs[p][...] for p in range(n_pages)], axis=0)
    hi, mid, lo = _split3(lf_all)
    cs = _dot(jnp.concatenate([hi, mid, lo], axis=0).astype(BF16), tri_ref[...])
    np8 = 8 * n_pages
    c_in = cs[0:np8] + cs[np8:2 * np8] + cs[2 * np8:3 * np8]
    carry = jnp.zeros((8, 1), F32)
    cks = []
    for p in range(n_pages):
        cks.append(c_in[8 * p:8 * p + 8] + carry)
        carry = carry + c_in[8 * p:8 * p + 8, PAGE_SIZE - 1:PAGE_SIZE]
    jj = lax.broadcasted_iota(jnp.int32, (n_tok, LANES), 0)
    tt = lax.broadcasted_iota(jnp.int32, (n_tok, LANES), 1)
    sel = ((jj >= b * t_new) & (jj <= b * t_new + tt) & (tt < t_new)).astype(BF16)
    cn = _cumsum_lanes(lfT_ref[...], sel) + carry
    cq = jnp.concatenate([cn[:, i:i + 1] for i in range(t_new)], axis=0)

    ss = [_dot(qbd, k_refs[p][...].astype(BF16)) + (cq - jnp.concatenate([cks[p]] * t_new, axis=0))
          for p in range(n_pages)]
    pad = jnp.zeros((16 - t_new, D_FOX), F32)
    knew = jnp.concatenate([knew_ref[...], pad], axis=0).astype(BF16)
    vnew = jnp.concatenate([vnew_ref[...], pad], axis=0).astype(BF16)
    sn = _dot(qbd, knew, _NT) + (cq - jnp.concatenate([cn[:, 0:16]] * t_new, axis=0))
    qt = lax.broadcasted_iota(jnp.int32, (nrow, 16), 0) // 8
    kt = lax.broadcasted_iota(jnp.int32, (nrow, 16), 1)
    sn = jnp.where(kt <= qt, sn, NEG)

    smax = ss[0]
    for s in ss[1:]:
        smax = jnp.maximum(smax, s)
    m = jnp.maximum(jnp.max(smax, axis=-1, keepdims=True), jnp.max(sn, axis=-1, keepdims=True))
    en = jnp.exp(sn - m)
    acc = _dot(en.astype(BF16), vnew)
    esum = jnp.zeros((nrow, PAGE_SIZE), F32)
    for p in range(n_pages):
        e = jnp.exp(ss[p] - m)
        esum = esum + e
        acc = acc + _dot(e.astype(BF16), v_refs[p][...].astype(BF16), _NT)
    l = jnp.sum(esum, axis=-1, keepdims=True) + jnp.sum(en, axis=-1, keepdims=True)
    fo_ref[...] = _collapse_rows(acc * (1.0 / l), t_new)

    qx = rest_ref[:, 0:D_XATTN]
    qxbd = _expand_rows(qx, D_XATTN).astype(BF16)
    sx = _dot(qxbd, mkT_ref[...].astype(BF16)) * SCALE
    ex = jnp.exp(sx - jnp.max(sx, axis=-1, keepdims=True))
    px = (ex / jnp.sum(ex, axis=-1, keepdims=True)).astype(BF16)
    xo_ref[...] = _collapse_rows(_dot(px, mvT_ref[...].astype(BF16), _NT), t_new)


def _decode(layer, page_table, q, knew, vnew, lfT, rest, kc, vc, lfc, mkc, mvc, tri_page):
    nb, t_new, _ = q.shape
    n_pages = page_table.shape[1]
    row = lambda width: pl.BlockSpec((None, t_new, width), lambda bi, pt: (bi, 0, 0))
    page = lambda rows, p: pl.BlockSpec((None, None, rows, PAGE_SIZE), lambda bi, pt: (layer, pt[bi, p], 0, 0))
    mem = pl.BlockSpec((None, None, D_XATTN, N_MEM), lambda bi, pt: (layer, bi, 0, 0))
    in_specs = [row(D_FOX), row(D_FOX), row(D_FOX),
                pl.BlockSpec(lfT.shape, lambda bi, pt: (0, 0)),
                row(D_REST), mem, mem,
                pl.BlockSpec(tri_page.shape, lambda bi, pt: (0, 0))]
    in_specs += [page(D_FOX, p) for p in range(n_pages)]
    in_specs += [page(D_FOX, p) for p in range(n_pages)]
    in_specs += [page(H_FOX, p) for p in range(n_pages)]
    return pl.pallas_call(
        functools.partial(_decode_body, n_pages, t_new),
        grid_spec=pltpu.PrefetchScalarGridSpec(
            num_scalar_prefetch=1, grid=(nb,), in_specs=in_specs,
            out_specs=[row(D_FOX), row(D_XATTN)]),
        out_shape=[jax.ShapeDtypeStruct((nb, t_new, D_FOX), F32),
                   jax.ShapeDtypeStruct((nb, t_new, D_XATTN), F32)],
        compiler_params=pltpu.CompilerParams(
            dimension_semantics=("arbitrary",), vmem_limit_bytes=VMEM_LIMIT),
        name="decode",
    )(page_table, q, knew, vnew, lfT, rest, mkc, mvc, tri_page, *([kc] * n_pages), *([vc] * n_pages),
      *([lfc] * n_pages))


def _finish_sample_body(t_new, wsp_ref, bsp_ref, x_ref, fo_ref, xo_ref, rest_ref, bn_ref, wout_ref,
                        gpost_ref, y_ref):
    n = x_ref.shape[0]
    u = rest_ref[:, D_XATTN:D_XATTN + D_GMLP]
    vg = rest_ref[:, D_XATTN + D_GMLP:D_XATTN + 2 * D_GMLP]
    gate = rest_ref[:, D_XATTN + 2 * D_GMLP:]

    pos = lax.broadcasted_iota(jnp.int32, (n, D_GMLP), 0) % t_new
    grp = lax.broadcasted_iota(jnp.int32, (n, D_GMLP), 1) // HEAD_DIM
    mixed = jnp.zeros((n, D_GMLP), F32)
    for g in range(G_GMLP):
        for i in range(t_new):
            mixed = jnp.where((grp == g) & (pos == i), bsp_ref[g, i], mixed)
    for k in range(t_new):
        coef = jnp.zeros((n, D_GMLP), F32)
        for g in range(G_GMLP):
            for i in range(k, t_new):
                coef = jnp.where((grp == g) & (pos == i), wsp_ref[g, i * t_new + i - k], coef)
        shifted = vg if k == 0 else pltpu.roll(vg, k, 0)
        mixed = mixed + coef * shifted
    go = u * mixed
    y_ref[...] = _merge_and_project(x_ref[...], fo_ref[...], go, xo_ref[...], gate, bn_ref, wout_ref,
                                    gpost_ref)


def _finish_sample(x, fo, xo, rest, w_small, b_small, w, t_new):
    n, d = x.shape
    full = lambda shape: pl.BlockSpec(shape, lambda i: (0,) * len(shape))
    smem = pl.BlockSpec(memory_space=pltpu.SMEM)
    return pl.pallas_call(
        functools.partial(_finish_sample_body, t_new),
        grid=(1,),
        in_specs=[smem, smem, full((n, d)), full((n, D_FOX)), full((n, D_XATTN)), full((n, D_REST)),
                  full((1, D_MIX)), full((D_MIX, d)), full((1, d))],
        out_specs=full((n, d)),
        out_shape=jax.ShapeDtypeStruct((n, d), F32),
        compiler_params=pltpu.CompilerParams(
            dimension_semantics=("arbitrary",), vmem_limit_bytes=VMEM_LIMIT),
        name="finish_sample",
    )(w_small, b_small, x, fo, xo, rest, w['bn'], w['wout'], w['gpost'])


def _constants(tm):
    f = jnp.arange(D_FOX)
    gsum = (f[:, None] // HEAD_DIM == f[None, :] // HEAD_DIM).astype(BF16)
    r = jnp.arange(32)
    fh, fo = f // HEAD_DIM, f % HEAD_DIM
    selq = jnp.where((fo[:, None] < 3) & (r[None, :] == fo[:, None] * 8 + fh[:, None]), 1.0,
                     jnp.where((fo[:, None] >= 3) & (fo[:, None] < 6) & (r[None, :] == 24), 1.0, 0.0)).astype(BF16)
    selk = jnp.where((fo[:, None] < 3) & (r[None, :] == 24), 1.0,
                     jnp.where((fo[:, None] >= 3) & (fo[:, None] < 6)
                               & (r[None, :] == (fo[:, None] - 3) * 8 + fh[:, None]), -1.0, 0.0)).astype(BF16)
    t = jnp.arange(tm)
    tri = (t[:, None] <= t[None, :]).astype(BF16)
    p = jnp.arange(PAGE_SIZE)
    tri_page = (p[:, None] <= p[None, :]).astype(BF16)
    return dict(gsum=gsum, selq=selq, selk=selk, tri=tri, tri_page=tri_page)


def _layer_weights(l, norm_pre, w_in, b_forget, q_norm, k_norm, gmlp_v_norm, w_spatial, b_spatial,
                   branch_norm, w_out, norm_post):
    d = w_in.shape[1]
    wl = w_in[l]
    qkv_cols = wl[:, 0:3 * D_FOX]
    rest_cols = wl[:, 3 * D_FOX:3 * D_FOX + D_REST]
    fl_cols = wl[:, 3 * D_FOX + D_REST:]
    wflT = jnp.concatenate([fl_cols.T, jnp.zeros((8, d), F32)], axis=0).astype(BF16)
    return dict(
        gpre=norm_pre[l].reshape(1, d),
        wa=rest_cols.astype(BF16),
        wqkvT=qkv_cols.T.astype(BF16),
        ws=wl[:, 0:3 * D_FOX + D_REST].astype(BF16),
        wflT=wflT,
        bf=b_forget[l].reshape(H_FOX, 1),
        qg=(q_norm[l] * SCALE).reshape(HEAD_DIM, 1),
        qgrow=(jnp.tile(q_norm[l], H_FOX) * SCALE).reshape(1, D_FOX),
        kg=k_norm[l].reshape(HEAD_DIM, 1),
        kgrow=jnp.tile(k_norm[l], H_FOX).reshape(1, D_FOX),
        vgain=gmlp_v_norm[l].reshape(1, D_GMLP),
        wsp=w_spatial[l].reshape(G_GMLP * CHUNK, CHUNK),
        bsp=jnp.repeat(b_spatial[l].T, HEAD_DIM, axis=1),
        bn=branch_norm[l].reshape(1, D_MIX),
        wout=w_out[l].astype(BF16),
        gpost=norm_post[l].reshape(1, d),
    )


def kernel(x_prompt, x_sample, mem_prompt, cache_fox_k, cache_fox_v, cache_fox_lf, cache_mem_k, cache_mem_v,
           page_table, norm_pre, w_in, b_forget, q_norm, k_norm, gmlp_v_norm, w_spatial, b_spatial, mem_norm,
           w_mem_kv, branch_norm, w_out, norm_post):
    depth = w_in.shape[0]
    b, s, d = x_prompt.shape
    nb, t_new, _ = x_sample.shape
    n_pool = cache_fox_k.shape[1]
    tm_proj = min(TM_PROJ, s)
    tm_finish = min(TM_FINISH, s)
    bq = min(BQ, s)
    consts = _constants(tm_proj)

    kc = jnp.transpose(cache_fox_k, (0, 1, 3, 4, 2)).reshape(depth, n_pool, D_FOX, PAGE_SIZE)
    vc = jnp.transpose(cache_fox_v, (0, 1, 3, 4, 2)).reshape(depth, n_pool, D_FOX, PAGE_SIZE)
    lfc = jnp.transpose(cache_fox_lf, (0, 1, 3, 2))
    mkc = jnp.transpose(cache_mem_k, (0, 1, 3, 4, 2)).reshape(depth, nb, D_XATTN, N_MEM)
    mvc = jnp.transpose(cache_mem_v, (0, 1, 3, 4, 2)).reshape(depth, nb, D_XATTN, N_MEM)

    wmT = jnp.transpose(w_mem_kv, (0, 2, 1)).astype(BF16)
    mkT, mvT, mkbd, mvbd = _mem_kv(mem_prompt, mem_norm, wmT)

    xp = x_prompt
    xs = x_sample.reshape(nb * t_new, d)
    kTs, vTs, lfTs, kss, vss, lfss, vgs = [], [], [], [], [], [], []
    for l in range(depth):
        w = _layer_weights(l, norm_pre, w_in, b_forget, q_norm, k_norm, gmlp_v_norm, w_spatial, b_spatial,
                           branch_norm, w_out, norm_post)
        qTa, ka, vTa, kT, vT, lfT, rest = _proj_prompt(xp, w, consts, tm_proj)
        o = _fox_prompt(qTa, ka, vTa, bq, bq)
        xp = _finish_prompt(xp, o, rest, mkbd, mvbd, l, w, tm_finish)
        kTs.append(kT); vTs.append(vT); lfTs.append(lfT)

        q_s, k_s, v_s, lfT_s, rest_s = _proj_sample(xs, w, consts)
        r3 = lambda a: a.reshape(nb, t_new, a.shape[-1])
        fo, xo = _decode(l, page_table, r3(q_s), r3(k_s), r3(v_s), lfT_s, r3(rest_s), kc, vc, lfc, mkc, mvc,
                         consts['tri_page'])
        w_small = w_spatial[l][:, :t_new, :t_new].reshape(G_GMLP, t_new * t_new)
        b_small = b_spatial[l][:, :t_new]
        xs = _finish_sample(xs, fo.reshape(nb * t_new, D_FOX), xo.reshape(nb * t_new, D_XATTN), rest_s,
                            w_small, b_small, w, t_new)
        kss.append(k_s); vss.append(v_s); lfss.append(lfT_s)
        vgs.append(rest_s[:, D_XATTN + D_GMLP:D_XATTN + 2 * D_GMLP])

    def heads_last(ts):
        return jnp.transpose(jnp.stack(ts), (0, 1, 4, 2, 3))

    mem_view = lambda m: jnp.transpose(m.reshape(depth, b, H_XATTN, HEAD_DIM, N_MEM), (0, 1, 4, 2, 3))
    return (
        xp,
        xs.reshape(nb, t_new, d),
        heads_last(kTs),
        heads_last(vTs),
        jnp.transpose(jnp.stack(lfTs), (0, 1, 3, 2)),
        mem_view(mkT),
        mem_view(mvT),
        jnp.stack(kss).reshape(depth, nb, t_new, H_FOX, HEAD_DIM),
        jnp.stack(vss).reshape(depth, nb, t_new, H_FOX, HEAD_DIM),
        jnp.transpose(jnp.stack(lfss), (0, 2, 1)).reshape(depth, nb, t_new, H_FOX),
        jnp.stack(vgs).reshape(depth, nb, t_new, D_GMLP),
    )
```

```python
import functools

import jax
import jax.numpy as jnp
from jax import lax
from jax.experimental import pallas as pl
from jax.experimental.pallas import tpu as pltpu

HEAD_DIM = 64
H_FOX = 8
D_FOX = H_FOX * HEAD_DIM
G_GMLP = 4
D_GMLP = G_GMLP * HEAD_DIM
H_XATTN = 4
D_XATTN = H_XATTN * HEAD_DIM
D_MIX = D_FOX + D_GMLP + D_XATTN
N_MEM = 256
CHUNK = 128
PAGE_SIZE = 128
EPS = 1e-6
NEG = -1e30
SCALE = HEAD_DIM ** -0.5
LOG2E = 1.4426950408889634
LANES = 128
AUG = 2 * HEAD_DIM
D_REST = D_XATTN + 2 * D_GMLP + D_MIX
VMEM_LIMIT = 56 * 1024 * 1024
TM_PROJ = 512
TM_FINISH = 512
BQ = 512
QK_AHEAD = 2

F32 = jnp.float32
BF16 = jnp.bfloat16

_NN = (((1,), (0,)), ((), ()))
_NT = (((1,), (1,)), ((), ()))
_TN = (((0,), (0,)), ((), ()))


def _dot(a, b, dims=_NN):
    return lax.dot_general(a, b, dims, preferred_element_type=F32)


def _split3(x):
    hi = x.astype(BF16).astype(F32)
    r = x - hi
    mid = r.astype(BF16).astype(F32)
    lo = (r - mid).astype(BF16).astype(F32)
    return hi, mid, lo


def _log_sigmoid(x):
    return jnp.minimum(x, 0.0) - jnp.log1p(jnp.exp(-jnp.abs(x)))


def _rms(x, gain):
    return x * lax.rsqrt(jnp.mean(x * x, axis=-1, keepdims=True) + EPS) * gain


def _cumsum_lanes(xT, tri):
    hi, mid, lo = _split3(xT)
    parts = jnp.concatenate([hi, mid, lo, jnp.zeros_like(hi)], axis=0).astype(BF16)
    cs = _dot(parts, tri)
    return cs[0:8] + cs[8:16] + cs[16:24]


def _proj_prompt_body(n_carried, x_ref, gpre_ref, wT_ref, wflT_ref, bf_ref, qg_ref, kg_ref, vgain_ref,
                      selq_ref, selk_ref, tri_ref, *refs):
    qTa_ref, ka_ref, vTa_ref, kT_ref, vT_ref, lfT_ref, rest_ref, carry_ref = refs[n_carried:]
    tm = x_ref.shape[1]

    @pl.when(pl.program_id(1) == 0)
    def _():
        carry_ref[...] = jnp.zeros_like(carry_ref)

    x = x_ref[0]
    h = _rms(x, gpre_ref[...]).astype(BF16)

    za = _dot(h, wT_ref[3 * D_FOX:3 * D_FOX + D_REST], _NT)
    rest_ref[0, :, 0:D_XATTN + D_GMLP] = za[:, 0:D_XATTN + D_GMLP]
    vg = za[:, D_XATTN + D_GMLP:D_XATTN + 2 * D_GMLP]
    rest_ref[0, :, D_XATTN + D_GMLP:D_XATTN + 2 * D_GMLP] = _rms(vg, vgain_ref[...])
    rest_ref[0, :, D_XATTN + 2 * D_GMLP:] = za[:, D_XATTN + 2 * D_GMLP:]

    zT = _dot(wT_ref[0:3 * D_FOX], h, _NT)
    flT = _dot(wflT_ref[...], h, _NT)
    lfT = _log_sigmoid(flT[0:8] + bf_ref[...])
    lfT_ref[0] = lfT

    cT = _cumsum_lanes(lfT, tri_ref[...]) + carry_ref[:, 0:1]
    carry_ref[...] = jnp.broadcast_to(cT[:, tm - 1:tm], carry_ref.shape)
    chi, cmid, clo = _split3(cT * LOG2E)
    cparts = jnp.concatenate([chi, cmid, clo, jnp.ones_like(chi)], axis=0).astype(BF16)

    qaugT = _dot(selq_ref[...], cparts)
    kaugT = _dot(selk_ref[...], cparts)
    row = lax.broadcasted_iota(jnp.int32, (HEAD_DIM, tm), 0)
    vaugT = jnp.where(row == 0, 1.0, 0.0).astype(F32)

    def head_norm(t, gain):
        return t * lax.rsqrt(jnp.mean(t * t, axis=0, keepdims=True) + EPS) * gain

    for hh in range(H_FOX):
        sl = slice(hh * HEAD_DIM, (hh + 1) * HEAD_DIM)
        qn = head_norm(zT[sl], qg_ref[...])
        qTa_ref[0, hh] = jnp.concatenate([qn, qaugT[sl]], axis=0).astype(BF16)
        kn = head_norm(zT[D_FOX + hh * HEAD_DIM:D_FOX + (hh + 1) * HEAD_DIM], kg_ref[...])
        kT_ref[0, hh] = kn
        ka_ref[0, hh] = jnp.concatenate([kn, kaugT[sl]], axis=0).T.astype(BF16)
        vh = zT[2 * D_FOX + hh * HEAD_DIM:2 * D_FOX + (hh + 1) * HEAD_DIM]
        vT_ref[0, hh] = vh
        vTa_ref[0, hh] = jnp.concatenate([vh, vaugT], axis=0).astype(BF16)


def _proj_prompt(x, layer, depth, wT, w, consts, tm, carried):
    b, s, d = x.shape
    n_t = s // tm
    full = lambda shape: pl.BlockSpec(shape, lambda bi, ti: (0,) * len(shape), pipeline_mode=pl.Buffered(1))
    in_specs = [
        pl.BlockSpec((1, tm, d), lambda bi, ti: (bi, ti, 0)),
        full((1, d)),
        pl.BlockSpec((None, 3 * D_FOX + D_REST, d), lambda bi, ti: (layer, 0, 0), pipeline_mode=pl.Buffered(1)),
        full(w['wflT'].shape),
        full((8, 1)), full((HEAD_DIM, 1)), full((HEAD_DIM, 1)), full((1, D_GMLP)),
        full(consts['selq'].shape), full(consts['selk'].shape), full(consts['tri'].shape),
    ]
    operands = [x, w['gpre'], wT, w['wflT'], w['bf'], w['qg'], w['kg'], w['vgain'],
                consts['selq'], consts['selk'], consts['tri']]
    aliases = {}
    if carried is not None:
        for n, buf in enumerate(carried):
            aliases[len(operands)] = 3 + n
            in_specs.append(pl.BlockSpec(memory_space=pl.ANY))
            operands.append(buf)
    out_shape = [
        jax.ShapeDtypeStruct((b, H_FOX, AUG, s), BF16),
        jax.ShapeDtypeStruct((b, H_FOX, s, AUG), BF16),
        jax.ShapeDtypeStruct((b, H_FOX, AUG, s), BF16),
        jax.ShapeDtypeStruct((depth, b, H_FOX, HEAD_DIM, s), F32),
        jax.ShapeDtypeStruct((depth, b, H_FOX, HEAD_DIM, s), F32),
        jax.ShapeDtypeStruct((depth, b, H_FOX, s), F32),
        jax.ShapeDtypeStruct((b, s, D_REST), F32),
    ]
    out_specs = [
        pl.BlockSpec((1, H_FOX, AUG, tm), lambda bi, ti: (bi, 0, 0, ti)),
        pl.BlockSpec((1, H_FOX, tm, AUG), lambda bi, ti: (bi, 0, ti, 0)),
        pl.BlockSpec((1, H_FOX, AUG, tm), lambda bi, ti: (bi, 0, 0, ti)),
        pl.BlockSpec((None, 1, H_FOX, HEAD_DIM, tm), lambda bi, ti: (layer, bi, 0, 0, ti)),
        pl.BlockSpec((None, 1, H_FOX, HEAD_DIM, tm), lambda bi, ti: (layer, bi, 0, 0, ti)),
        pl.BlockSpec((None, 1, H_FOX, tm), lambda bi, ti: (layer, bi, 0, ti)),
        pl.BlockSpec((1, tm, D_REST), lambda bi, ti: (bi, ti, 0)),
    ]
    return pl.pallas_call(
        functools.partial(_proj_prompt_body, len(aliases)),
        grid=(b, n_t),
        in_specs=in_specs, out_specs=out_specs, out_shape=out_shape,
        scratch_shapes=[pltpu.VMEM((8, LANES), F32)],
        input_output_aliases=aliases,
        compiler_params=pltpu.CompilerParams(
            dimension_semantics=("arbitrary", "arbitrary"), vmem_limit_bytes=VMEM_LIMIT),
        name="proj_prompt",
    )(*operands)


def _fox_prompt_body(qi_ref, kj_ref, qTa_ref, ka_ref, vTa_ref, o_ref, m_ref, acc_ref):
    bq = qTa_ref.shape[3]
    bk = ka_ref.shape[2]
    i = qi_ref[pl.program_id(1)]
    j = kj_ref[pl.program_id(1)]

    @pl.when(j == 0)
    def _():
        m_ref[...] = jnp.full_like(m_ref, NEG)
        acc_ref[...] = jnp.zeros_like(acc_ref)

    def all_heads(diagonal):
        if diagonal:
            keep = (lax.broadcasted_iota(jnp.int32, (bk, bq), 0)
                    <= lax.broadcasted_iota(jnp.int32, (bk, bq), 1))
        sTs = {h: _dot(ka_ref[0, h], qTa_ref[0, h]) for h in range(QK_AHEAD)}
        for hh in range(H_FOX):
            sT = sTs.pop(hh)
            if diagonal:
                sT = jnp.where(keep, sT, NEG)
            m_prev = m_ref[hh][0:1]
            m_new = jnp.maximum(m_prev, jnp.max(sT, axis=0, keepdims=True))
            alpha = jnp.exp2(m_prev - m_new)
            pT = jnp.exp2(sT - m_new).astype(BF16)
            if hh + QK_AHEAD < H_FOX:
                sTs[hh + QK_AHEAD] = _dot(ka_ref[0, hh + QK_AHEAD], qTa_ref[0, hh + QK_AHEAD])
            pv = _dot(vTa_ref[0, hh], pT)
            acc_ref[hh] = alpha * acc_ref[hh] + pv
            m_ref[hh] = jnp.broadcast_to(m_new, (8, bq))

    @pl.when(j < i)
    def _():
        all_heads(False)

    @pl.when(j == i)
    def _():
        all_heads(True)
        lane = lax.broadcasted_iota(jnp.int32, (bq, LANES), 1)
        for pair in range(H_FOX // 2):
            a0 = acc_ref[2 * pair]
            a1 = acc_ref[2 * pair + 1]
            o0 = (a0 * (1.0 / a0[HEAD_DIM:HEAD_DIM + 1])).T
            o1 = (a1 * (1.0 / a1[HEAD_DIM:HEAD_DIM + 1])).T
            o_ref[0, :, pair * LANES:(pair + 1) * LANES] = jnp.where(
                lane < HEAD_DIM, o0, pltpu.roll(o1, HEAD_DIM, 1))


def _fox_prompt(qTa, ka, vTa, bq):
    b, _, _, s = qTa.shape
    nq = s // bq
    pairs = [(i, j) for i in range(nq) for j in range(i + 1)]
    qi = jnp.array([p[0] for p in pairs], jnp.int32)
    kj = jnp.array([p[1] for p in pairs], jnp.int32)
    return pl.pallas_call(
        _fox_prompt_body,
        grid_spec=pltpu.PrefetchScalarGridSpec(
            num_scalar_prefetch=2,
            grid=(b, len(pairs)),
            in_specs=[
                pl.BlockSpec((1, H_FOX, AUG, bq), lambda bi, t, qi, kj: (bi, 0, 0, qi[t])),
                pl.BlockSpec((1, H_FOX, bq, AUG), lambda bi, t, qi, kj: (bi, 0, kj[t], 0)),
                pl.BlockSpec((1, H_FOX, AUG, bq), lambda bi, t, qi, kj: (bi, 0, 0, kj[t])),
            ],
            out_specs=pl.BlockSpec((1, bq, D_FOX), lambda bi, t, qi, kj: (bi, qi[t], 0)),
            scratch_shapes=[pltpu.VMEM((H_FOX, 8, bq), F32), pltpu.VMEM((H_FOX, AUG, bq), F32)]),
        out_shape=jax.ShapeDtypeStruct((b, s, D_FOX), F32),
        compiler_params=pltpu.CompilerParams(
            dimension_semantics=("arbitrary", "arbitrary"), vmem_limit_bytes=VMEM_LIMIT),
        name="fox_prompt",
    )(qi, kj, qTa, ka, vTa)


def _mem_kv_body(mem_ref, gain_ref, wT_ref, mkT_ref, mvT_ref, mkbd_ref, mvbd_ref):
    hm = _rms(mem_ref[0], gain_ref[0]).astype(BF16)
    kvT = _dot(wT_ref[0], hm, _NT)
    mkT = kvT[:D_XATTN]
    mvT = kvT[D_XATTN:]
    mkT_ref[0, 0] = mkT
    mvT_ref[0, 0] = mvT
    mkbd_ref[...] = jnp.zeros_like(mkbd_ref)
    mvbd_ref[...] = jnp.zeros_like(mvbd_ref)
    for hh in range(H_XATTN):
        r = slice(hh * HEAD_DIM, (hh + 1) * HEAD_DIM)
        c = slice(hh * N_MEM, (hh + 1) * N_MEM)
        mkbd_ref[0, 0, r, c] = mkT[r].astype(BF16)
        mvbd_ref[0, 0, r, c] = mvT[r].astype(BF16)


def _mem_kv(mem, mem_norm, wmT):
    b, n_mem, d = mem.shape
    depth = wmT.shape[0]
    blk = lambda shape: pl.BlockSpec((1, 1) + shape, lambda l, bi: (l, bi, 0, 0))
    return pl.pallas_call(
        _mem_kv_body,
        grid=(depth, b),
        in_specs=[
            pl.BlockSpec((1, n_mem, d), lambda l, bi: (bi, 0, 0)),
            pl.BlockSpec((1, 1, d), lambda l, bi: (l, 0, 0)),
            pl.BlockSpec((1, 2 * D_XATTN, d), lambda l, bi: (l, 0, 0)),
        ],
        out_specs=[blk((D_XATTN, n_mem)), blk((D_XATTN, n_mem)),
                   blk((D_XATTN, H_XATTN * n_mem)), blk((D_XATTN, H_XATTN * n_mem))],
        out_shape=[
            jax.ShapeDtypeStruct((depth, b, D_XATTN, n_mem), F32),
            jax.ShapeDtypeStruct((depth, b, D_XATTN, n_mem), F32),
            jax.ShapeDtypeStruct((depth, b, D_XATTN, H_XATTN * n_mem), BF16),
            jax.ShapeDtypeStruct((depth, b, D_XATTN, H_XATTN * n_mem), BF16),
        ],
        compiler_params=pltpu.CompilerParams(
            dimension_semantics=("arbitrary", "arbitrary"), vmem_limit_bytes=VMEM_LIMIT),
        name="mem_kv",
    )(mem, mem_norm.reshape(depth, 1, d), wmT)


def _merge_and_project(x, fo, go, xo, gate, bn_ref, wout_ref, gpost_ref):
    merged = jnp.concatenate([
        _rms(fo, bn_ref[:, 0:D_FOX]),
        _rms(go, bn_ref[:, D_FOX:D_FOX + D_GMLP]),
        _rms(xo, bn_ref[:, D_FOX + D_GMLP:]),
    ], axis=-1) * (gate * jax.nn.sigmoid(gate))
    y = _dot(merged.astype(BF16), wout_ref[...])
    return x + _rms(y, gpost_ref[...])


def _finish_prompt_body(x_ref, o_ref, rest_ref, mkbd_ref, mvbd_ref, wsp_ref, bsp_ref, bn_ref, wout_ref,
                        gpost_ref, y_ref):
    tm = x_ref.shape[1]
    rest = rest_ref[0]
    qx = rest[:, 0:D_XATTN]
    u = rest[:, D_XATTN:D_XATTN + D_GMLP]
    vg = rest[:, D_XATTN + D_GMLP:D_XATTN + 2 * D_GMLP]
    gate = rest[:, D_XATTN + 2 * D_GMLP:]

    wi = lax.broadcasted_iota(jnp.int32, wsp_ref.shape, 0) % CHUNK
    wj = lax.broadcasted_iota(jnp.int32, wsp_ref.shape, 1)
    w_tril = jnp.where(wj <= wi, wsp_ref[...], 0.0).astype(BF16)
    grp = lax.broadcasted_iota(jnp.int32, (CHUNK, D_GMLP), 1) // HEAD_DIM
    gos = []
    for c in range(tm // CHUNK):
        rs = slice(c * CHUNK, (c + 1) * CHUNK)
        mall = _dot(w_tril, vg[rs].astype(BF16))
        mixed = mall[0:CHUNK]
        for g in range(1, G_GMLP):
            mixed = jnp.where(grp == g, mall[g * CHUNK:(g + 1) * CHUNK], mixed)
        gos.append(u[rs] * (mixed + bsp_ref[...]))
    go = jnp.concatenate(gos, axis=0)

    s = _dot(qx.astype(BF16), mkbd_ref[0, 0]) * SCALE
    ps = []
    for hh in range(H_XATTN):
        sh = s[:, hh * N_MEM:(hh + 1) * N_MEM]
        e = jnp.exp(sh - jnp.max(sh, axis=-1, keepdims=True))
        ps.append(e / jnp.sum(e, axis=-1, keepdims=True))
    p = jnp.concatenate(ps, axis=-1).astype(BF16)
    xo = _dot(p, mvbd_ref[0, 0], _NT)

    y_ref[0] = _merge_and_project(x_ref[0], o_ref[0], go, xo, gate, bn_ref, wout_ref, gpost_ref)


def _finish_prompt(x, o, rest, mkbd, mvbd, layer, w, tm):
    b, s, d = x.shape
    full = lambda shape: pl.BlockSpec(shape, lambda bi, ti: (0,) * len(shape))
    tile = lambda width: pl.BlockSpec((1, tm, width), lambda bi, ti: (bi, ti, 0))
    kvspec = pl.BlockSpec((1, 1, D_XATTN, H_XATTN * N_MEM), lambda bi, ti: (layer, bi, 0, 0))
    return pl.pallas_call(
        _finish_prompt_body,
        grid=(b, s // tm),
        in_specs=[tile(d), tile(D_FOX), tile(D_REST), kvspec, kvspec,
                  full((G_GMLP * CHUNK, CHUNK)), full((CHUNK, D_GMLP)), full((1, D_MIX)),
                  full((D_MIX, d)), full((1, d))],
        out_specs=tile(d),
        out_shape=jax.ShapeDtypeStruct((b, s, d), F32),
        compiler_params=pltpu.CompilerParams(
            dimension_semantics=("arbitrary", "arbitrary"), vmem_limit_bytes=VMEM_LIMIT),
        name="finish_prompt",
    )(x, o, rest, mkbd, mvbd, w['wsp'], w['bsp'], w['bn'], w['wout'], w['gpost'])


def _proj_sample_body(x_ref, gpre_ref, wT_ref, wflT_ref, bf_ref, qg_ref, kgrow_ref, vgain_ref, gsum_ref,
                      q_ref, k_ref, v_ref, lfT_ref, rest_ref):
    h = _rms(x_ref[...], gpre_ref[...]).astype(BF16)
    z = _dot(h, wT_ref[...], _NT)

    def head_norm(t, gain):
        t2 = t * t
        t2h = t2.astype(BF16)
        t2l = (t2 - t2h.astype(F32)).astype(BF16)
        msq = (_dot(t2h, gsum_ref[...]) + _dot(t2l, gsum_ref[...])) * (1.0 / HEAD_DIM)
        return t * lax.rsqrt(msq + EPS) * gain

    q_ref[...] = head_norm(z[:, 0:D_FOX], qg_ref[...]).astype(BF16)
    k_ref[...] = head_norm(z[:, D_FOX:2 * D_FOX], kgrow_ref[...])
    v_ref[...] = z[:, 2 * D_FOX:3 * D_FOX]
    flT = _dot(wflT_ref[...], h, _NT)
    lfT_ref[...] = _log_sigmoid(flT[0:8] + bf_ref[...])
    base = 3 * D_FOX
    rest_ref[:, 0:D_XATTN + D_GMLP] = z[:, base:base + D_XATTN + D_GMLP]
    vg = z[:, base + D_XATTN + D_GMLP:base + D_XATTN + 2 * D_GMLP]
    rest_ref[:, D_XATTN + D_GMLP:D_XATTN + 2 * D_GMLP] = _rms(vg, vgain_ref[...])
    rest_ref[:, D_XATTN + 2 * D_GMLP:] = z[:, base + D_XATTN + 2 * D_GMLP:]


def _proj_sample(x, layer, wT, w, consts):
    n, d = x.shape
    full = lambda shape: pl.BlockSpec(shape, lambda i: (0,) * len(shape))
    return pl.pallas_call(
        _proj_sample_body,
        grid=(1,),
        in_specs=[full((n, d)), full((1, d)),
                  pl.BlockSpec((None, 3 * D_FOX + D_REST, d), lambda i: (layer, 0, 0)),
                  full(w['wflT'].shape), full((8, 1)),
                  full((1, D_FOX)), full((1, D_FOX)), full((1, D_GMLP)), full(consts['gsum'].shape)],
        out_specs=[full((n, D_FOX)), full((n, D_FOX)), full((n, D_FOX)), full((8, n)), full((n, D_REST))],
        out_shape=[
            jax.ShapeDtypeStruct((n, D_FOX), BF16),
            jax.ShapeDtypeStruct((n, D_FOX), F32),
            jax.ShapeDtypeStruct((n, D_FOX), F32),
            jax.ShapeDtypeStruct((8, n), F32),
            jax.ShapeDtypeStruct((n, D_REST), F32),
        ],
        compiler_params=pltpu.CompilerParams(
            dimension_semantics=("arbitrary",), vmem_limit_bytes=VMEM_LIMIT),
        name="proj_sample",
    )(x, w['gpre'], wT, w['wflT'], w['bf'], w['qgrow'], w['kgrow'], w['vgain'], consts['gsum'])


def _expand_rows(x4, width):
    t = x4.shape[0]
    rep = jnp.concatenate([jnp.broadcast_to(x4[i:i + 1], (8, width)) for i in range(t)], axis=0)
    rowh = lax.broadcasted_iota(jnp.int32, (8 * t, width), 0) % 8
    colh = lax.broadcasted_iota(jnp.int32, (8 * t, width), 1) // HEAD_DIM
    return jnp.where(rowh == colh, rep, jnp.zeros_like(rep))


def _collapse_rows(x, t):
    width = x.shape[1]
    rowh = lax.broadcasted_iota(jnp.int32, x.shape, 0) % 8
    colh = lax.broadcasted_iota(jnp.int32, x.shape, 1) // HEAD_DIM
    xm = jnp.where(rowh == colh, x, 0.0)
    return jnp.concatenate([jnp.sum(xm[i * 8:(i + 1) * 8], axis=0, keepdims=True) for i in range(t)], axis=0)


def _decode_body(n_pages, t_new, pt_ref, q_ref, knew_ref, vnew_ref, lfT_ref, rest_ref, mkT_ref, mvT_ref,
                 tri_ref, *refs):
    k_refs = refs[0:n_pages]
    v_refs = refs[n_pages:2 * n_pages]
    lf_refs = refs[2 * n_pages:3 * n_pages]
    fo_ref, xo_ref = refs[3 * n_pages:]
    b = pl.program_id(0)
    nrow = 8 * t_new
    n_tok = lfT_ref.shape[1]

    qbd = _expand_rows(q_ref[...].astype(F32), D_FOX).astype(BF16)

    lf_all = jnp.concatenate([lf_refs[p][...] for p in range(n_pages)], axis=0)
    hi, mid, lo = _split3(lf_all)
    cs = _dot(jnp.concatenate([hi, mid, lo], axis=0).astype(BF16), tri_ref[...])
    np8 = 8 * n_pages
    c_in = cs[0:np8] + cs[np8:2 * np8] + cs[2 * np8:3 * np8]
    carry = jnp.zeros((8, 1), F32)
    cks = []
    for p in range(n_pages):
        cks.append(c_in[8 * p:8 * p + 8] + carry)
        carry = carry + c_in[8 * p:8 * p + 8, PAGE_SIZE - 1:PAGE_SIZE]
    jj = lax.broadcasted_iota(jnp.int32, (n_tok, LANES), 0)
    tt = lax.broadcasted_iota(jnp.int32, (n_tok, LANES), 1)
    sel = ((jj >= b * t_new) & (jj <= b * t_new + tt) & (tt < t_new)).astype(BF16)
    cn = _cumsum_lanes(lfT_ref[...], sel) + carry
    cq = jnp.concatenate([cn[:, i:i + 1] for i in range(t_new)], axis=0)

    ss = [_dot(qbd, k_refs[p][...].astype(BF16)) + (cq - jnp.concatenate([cks[p]] * t_new, axis=0))
          for p in range(n_pages)]
    pad = jnp.zeros((16 - t_new, D_FOX), F32)
    knew = jnp.concatenate([knew_ref[...], pad], axis=0).astype(BF16)
    vnew = jnp.concatenate([vnew_ref[...], pad], axis=0).astype(BF16)
    sn = _dot(qbd, knew, _NT) + (cq - jnp.concatenate([cn[:, 0:16]] * t_new, axis=0))
    qt = lax.broadcasted_iota(jnp.int32, (nrow, 16), 0) // 8
    kt = lax.broadcasted_iota(jnp.int32, (nrow, 16), 1)
    sn = jnp.where(kt <= qt, sn, NEG)

    smax = ss[0]
    for s in ss[1:]:
        smax = jnp.maximum(smax, s)
    m = jnp.maximum(jnp.max(smax, axis=-1, keepdims=True), jnp.max(sn, axis=-1, keepdims=True))
    en = jnp.exp(sn - m)
    acc = _dot(en.astype(BF16), vnew)
    esum = jnp.zeros((nrow, PAGE_SIZE), F32)
    for p in range(n_pages):
        e = jnp.exp(ss[p] - m)
        esum = esum + e
        acc = acc + _dot(e.astype(BF16), v_refs[p][...].astype(BF16), _NT)
    l = jnp.sum(esum, axis=-1, keepdims=True) + jnp.sum(en, axis=-1, keepdims=True)
    fo_ref[...] = _collapse_rows(acc * (1.0 / l), t_new)

    qx = rest_ref[:, 0:D_XATTN]
    qxbd = _expand_rows(qx, D_XATTN).astype(BF16)
    sx = _dot(qxbd, mkT_ref[...].astype(BF16)) * SCALE
    ex = jnp.exp(sx - jnp.max(sx, axis=-1, keepdims=True))
    px = (ex / jnp.sum(ex, axis=-1, keepdims=True)).astype(BF16)
    xo_ref[...] = _collapse_rows(_dot(px, mvT_ref[...].astype(BF16), _NT), t_new)


def _decode(layer, page_table, q, knew, vnew, lfT, rest, kc, vc, lfc, mkc, mvc, tri_page):
    nb, t_new, _ = q.shape
    n_pages = page_table.shape[1]
    row = lambda width: pl.BlockSpec((None, t_new, width), lambda bi, pt: (bi, 0, 0))
    page = lambda rows, p: pl.BlockSpec((None, None, rows, PAGE_SIZE), lambda bi, pt: (layer, pt[bi, p], 0, 0))
    mem = pl.BlockSpec((None, None, D_XATTN, N_MEM), lambda bi, pt: (layer, bi, 0, 0))
    in_specs = [row(D_FOX), row(D_FOX), row(D_FOX),
                pl.BlockSpec(lfT.shape, lambda bi, pt: (0, 0)),
                row(D_REST), mem, mem,
                pl.BlockSpec(tri_page.shape, lambda bi, pt: (0, 0))]
    in_specs += [page(D_FOX, p) for p in range(n_pages)]
    in_specs += [page(D_FOX, p) for p in range(n_pages)]
    in_specs += [page(H_FOX, p) for p in range(n_pages)]
    return pl.pallas_call(
        functools.partial(_decode_body, n_pages, t_new),
        grid_spec=pltpu.PrefetchScalarGridSpec(
            num_scalar_prefetch=1, grid=(nb,), in_specs=in_specs,
            out_specs=[row(D_FOX), row(D_XATTN)]),
        out_shape=[jax.ShapeDtypeStruct((nb, t_new, D_FOX), F32),
                   jax.ShapeDtypeStruct((nb, t_new, D_XATTN), F32)],
        compiler_params=pltpu.CompilerParams(
            dimension_semantics=("arbitrary",), vmem_limit_bytes=VMEM_LIMIT),
        name="decode",
    )(page_table, q, knew, vnew, lfT, rest, mkc, mvc, tri_page, *([kc] * n_pages), *([vc] * n_pages),
      *([lfc] * n_pages))


def _finish_sample_body(t_new, wsp_ref, bsp_ref, x_ref, fo_ref, xo_ref, rest_ref, bn_ref, wout_ref,
                        gpost_ref, y_ref):
    n = x_ref.shape[0]
    u = rest_ref[:, D_XATTN:D_XATTN + D_GMLP]
    vg = rest_ref[:, D_XATTN + D_GMLP:D_XATTN + 2 * D_GMLP]
    gate = rest_ref[:, D_XATTN + 2 * D_GMLP:]

    pos = lax.broadcasted_iota(jnp.int32, (n, D_GMLP), 0) % t_new
    grp = lax.broadcasted_iota(jnp.int32, (n, D_GMLP), 1) // HEAD_DIM
    mixed = jnp.zeros((n, D_GMLP), F32)
    for g in range(G_GMLP):
        for i in range(t_new):
            mixed = jnp.where((grp == g) & (pos == i), bsp_ref[g, i], mixed)
    for k in range(t_new):
        coef = jnp.zeros((n, D_GMLP), F32)
        for g in range(G_GMLP):
            for i in range(k, t_new):
                coef = jnp.where((grp == g) & (pos == i), wsp_ref[g, i * t_new + i - k], coef)
        shifted = vg if k == 0 else pltpu.roll(vg, k, 0)
        mixed = mixed + coef * shifted
    go = u * mixed
    y_ref[...] = _merge_and_project(x_ref[...], fo_ref[...], go, xo_ref[...], gate, bn_ref, wout_ref,
                                    gpost_ref)


def _finish_sample(x, fo, xo, rest, w_small, b_small, w, t_new):
    n, d = x.shape
    full = lambda shape: pl.BlockSpec(shape, lambda i: (0,) * len(shape))
    smem = pl.BlockSpec(memory_space=pltpu.SMEM)
    return pl.pallas_call(
        functools.partial(_finish_sample_body, t_new),
        grid=(1,),
        in_specs=[smem, smem, full((n, d)), full((n, D_FOX)), full((n, D_XATTN)), full((n, D_REST)),
                  full((1, D_MIX)), full((D_MIX, d)), full((1, d))],
        out_specs=full((n, d)),
        out_shape=jax.ShapeDtypeStruct((n, d), F32),
        compiler_params=pltpu.CompilerParams(
            dimension_semantics=("arbitrary",), vmem_limit_bytes=VMEM_LIMIT),
        name="finish_sample",
    )(w_small, b_small, x, fo, xo, rest, w['bn'], w['wout'], w['gpost'])


def _constants(tm):
    f = jnp.arange(D_FOX)
    gsum = (f[:, None] // HEAD_DIM == f[None, :] // HEAD_DIM).astype(BF16)
    r = jnp.arange(32)
    fh, fo = f // HEAD_DIM, f % HEAD_DIM
    selq = jnp.where((fo[:, None] < 3) & (r[None, :] == fo[:, None] * 8 + fh[:, None]), 1.0,
                     jnp.where((fo[:, None] >= 3) & (fo[:, None] < 6) & (r[None, :] == 24), 1.0, 0.0)).astype(BF16)
    selk = jnp.where((fo[:, None] < 3) & (r[None, :] == 24), 1.0,
                     jnp.where((fo[:, None] >= 3) & (fo[:, None] < 6)
                               & (r[None, :] == (fo[:, None] - 3) * 8 + fh[:, None]), -1.0, 0.0)).astype(BF16)
    t = jnp.arange(tm)
    tri = (t[:, None] <= t[None, :]).astype(BF16)
    p = jnp.arange(PAGE_SIZE)
    tri_page = (p[:, None] <= p[None, :]).astype(BF16)
    return dict(gsum=gsum, selq=selq, selk=selk, tri=tri, tri_page=tri_page)


def _layer_weights(l, norm_pre, w_in, b_forget, q_norm, k_norm, gmlp_v_norm, w_spatial, b_spatial,
                   branch_norm, w_out, norm_post):
    d = w_in.shape[1]
    fl_cols = w_in[l][:, 3 * D_FOX + D_REST:]
    wflT = jnp.concatenate([fl_cols.T, jnp.zeros((8, d), F32)], axis=0).astype(BF16)
    return dict(
        gpre=norm_pre[l].reshape(1, d),
        wflT=wflT,
        bf=b_forget[l].reshape(H_FOX, 1),
        qg=(q_norm[l] * (SCALE * LOG2E)).reshape(HEAD_DIM, 1),
        qgrow=(jnp.tile(q_norm[l], H_FOX) * SCALE).reshape(1, D_FOX),
        kg=k_norm[l].reshape(HEAD_DIM, 1),
        kgrow=jnp.tile(k_norm[l], H_FOX).reshape(1, D_FOX),
        vgain=gmlp_v_norm[l].reshape(1, D_GMLP),
        wsp=w_spatial[l].reshape(G_GMLP * CHUNK, CHUNK),
        bsp=jnp.repeat(b_spatial[l].T, HEAD_DIM, axis=1),
        bn=branch_norm[l].reshape(1, D_MIX),
        wout=w_out[l].astype(BF16),
        gpost=norm_post[l].reshape(1, d),
    )


def kernel(x_prompt, x_sample, mem_prompt, cache_fox_k, cache_fox_v, cache_fox_lf, cache_mem_k, cache_mem_v,
           page_table, norm_pre, w_in, b_forget, q_norm, k_norm, gmlp_v_norm, w_spatial, b_spatial, mem_norm,
           w_mem_kv, branch_norm, w_out, norm_post):
    depth = w_in.shape[0]
    b, s, d = x_prompt.shape
    nb, t_new, _ = x_sample.shape
    n_pool = cache_fox_k.shape[1]
    tm_proj = min(TM_PROJ, s)
    tm_finish = min(TM_FINISH, s)
    bq = min(BQ, s)
    consts = _constants(tm_proj)

    kc = jnp.transpose(cache_fox_k, (0, 1, 3, 4, 2)).reshape(depth, n_pool, D_FOX, PAGE_SIZE)
    vc = jnp.transpose(cache_fox_v, (0, 1, 3, 4, 2)).reshape(depth, n_pool, D_FOX, PAGE_SIZE)
    lfc = jnp.transpose(cache_fox_lf, (0, 1, 3, 2))
    mkc = jnp.transpose(cache_mem_k, (0, 1, 3, 4, 2)).reshape(depth, nb, D_XATTN, N_MEM)
    mvc = jnp.transpose(cache_mem_v, (0, 1, 3, 4, 2)).reshape(depth, nb, D_XATTN, N_MEM)

    wmT = jnp.transpose(w_mem_kv, (0, 2, 1)).astype(BF16)
    mkT, mvT, mkbd, mvbd = _mem_kv(mem_prompt, mem_norm, wmT)

    wT = jnp.transpose(w_in, (0, 2, 1)).astype(BF16)

    xp = x_prompt
    xs = x_sample.reshape(nb * t_new, d)
    carried = None
    kss, vss, lfss, vgs = [], [], [], []
    for l in range(depth):
        w = _layer_weights(l, norm_pre, w_in, b_forget, q_norm, k_norm, gmlp_v_norm, w_spatial, b_spatial,
                           branch_norm, w_out, norm_post)
        qTa, ka, vTa, kT_all, vT_all, lfT_all, rest = _proj_prompt(xp, l, depth, wT, w, consts, tm_proj, carried)
        carried = (kT_all, vT_all, lfT_all)
        o = _fox_prompt(qTa, ka, vTa, bq)
        xp = _finish_prompt(xp, o, rest, mkbd, mvbd, l, w, tm_finish)

        q_s, k_s, v_s, lfT_s, rest_s = _proj_sample(xs, l, wT, w, consts)
        r3 = lambda a: a.reshape(nb, t_new, a.shape[-1])
        fo, xo = _decode(l, page_table, r3(q_s), r3(k_s), r3(v_s), lfT_s, r3(rest_s), kc, vc, lfc, mkc, mvc,
                         consts['tri_page'])
        w_small = w_spatial[l][:, :t_new, :t_new].reshape(G_GMLP, t_new * t_new)
        b_small = b_spatial[l][:, :t_new]
        xs = _finish_sample(xs, fo.reshape(nb * t_new, D_FOX), xo.reshape(nb * t_new, D_XATTN), rest_s,
                            w_small, b_small, w, t_new)
        kss.append(k_s); vss.append(v_s); lfss.append(lfT_s)
        vgs.append(rest_s[:, D_XATTN + D_GMLP:D_XATTN + 2 * D_GMLP])

    heads_last = lambda t: jnp.transpose(t, (0, 1, 4, 2, 3))
    mem_view = lambda m: jnp.transpose(m.reshape(depth, b, H_XATTN, HEAD_DIM, N_MEM), (0, 1, 4, 2, 3))
    kT_all, vT_all, lfT_all = carried
    return (
        xp,
        xs.reshape(nb, t_new, d),
        heads_last(kT_all),
        heads_last(vT_all),
        jnp.transpose(lfT_all, (0, 1, 3, 2)),
        mem_view(mkT),
        mem_view(mvT),
        jnp.stack(kss).reshape(depth, nb, t_new, H_FOX, HEAD_DIM),
        jnp.stack(vss).reshape(depth, nb, t_new, H_FOX, HEAD_DIM),
        jnp.transpose(jnp.stack(lfss), (0, 2, 1)).reshape(depth, nb, t_new, H_FOX),
        jnp.stack(vgs).reshape(depth, nb, t_new, D_GMLP),
    )
```

```python
import functools

import jax
import jax.numpy as jnp
from jax import lax
from jax.experimental import pallas as pl
from jax.experimental.pallas import tpu as pltpu

HEAD_DIM = 64
H_FOX = 8
D_FOX = H_FOX * HEAD_DIM
G_GMLP = 4
D_GMLP = G_GMLP * HEAD_DIM
H_XATTN = 4
D_XATTN = H_XATTN * HEAD_DIM
D_MIX = D_FOX + D_GMLP + D_XATTN
N_MEM = 256
CHUNK = 128
PAGE_SIZE = 128
EPS = 1e-6
NEG = -1e30
SCALE = HEAD_DIM ** -0.5
LOG2E = 1.4426950408889634
LANES = 128
AUG = 2 * HEAD_DIM
D_REST = D_XATTN + 2 * D_GMLP + D_MIX
VMEM_LIMIT = 56 * 1024 * 1024
TM_PROJ = 512
TM_FINISH = 512
BQ = 512
QK_AHEAD = 2

F32 = jnp.float32
BF16 = jnp.bfloat16

_NN = (((1,), (0,)), ((), ()))
_NT = (((1,), (1,)), ((), ()))
_TN = (((0,), (0,)), ((), ()))


def _dot(a, b, dims=_NN):
    return lax.dot_general(a, b, dims, preferred_element_type=F32)


def _split3(x):
    hi = x.astype(BF16).astype(F32)
    r = x - hi
    mid = r.astype(BF16).astype(F32)
    lo = (r - mid).astype(BF16).astype(F32)
    return hi, mid, lo


def _log_sigmoid(x):
    return jnp.minimum(x, 0.0) - jnp.log1p(jnp.exp(-jnp.abs(x)))


def _rms(x, gain):
    return x * lax.rsqrt(jnp.mean(x * x, axis=-1, keepdims=True) + EPS) * gain


def _cumsum_lanes(xT, tri):
    hi, mid, lo = _split3(xT)
    parts = jnp.concatenate([hi, mid, lo, jnp.zeros_like(hi)], axis=0).astype(BF16)
    cs = _dot(parts, tri)
    return cs[0:8] + cs[8:16] + cs[16:24]


def _proj_prompt_body(n_carried, x_ref, gpre_ref, wT_ref, wflT_ref, bf_ref, qg_ref, kg_ref, vgain_ref,
                      selq_ref, selk_ref, tri_ref, *refs):
    qTa_ref, ka_ref, vTa_ref, kT_ref, vT_ref, lfT_ref, rest_ref, carry_ref = refs[n_carried:]
    tm = x_ref.shape[1]

    @pl.when(pl.program_id(1) == 0)
    def _():
        carry_ref[...] = jnp.zeros_like(carry_ref)

    x = x_ref[0]
    h = _rms(x, gpre_ref[...]).astype(BF16)

    za = _dot(h, wT_ref[3 * D_FOX:3 * D_FOX + D_REST], _NT)
    rest_ref[0, :, 0:D_XATTN + D_GMLP] = za[:, 0:D_XATTN + D_GMLP]
    vg = za[:, D_XATTN + D_GMLP:D_XATTN + 2 * D_GMLP]
    rest_ref[0, :, D_XATTN + D_GMLP:D_XATTN + 2 * D_GMLP] = _rms(vg, vgain_ref[...])
    rest_ref[0, :, D_XATTN + 2 * D_GMLP:] = za[:, D_XATTN + 2 * D_GMLP:]

    zT = _dot(wT_ref[0:3 * D_FOX], h, _NT)
    flT = _dot(wflT_ref[...], h, _NT)
    lfT = _log_sigmoid(flT[0:8] + bf_ref[...])
    lfT_ref[0] = lfT

    cT = _cumsum_lanes(lfT, tri_ref[...]) + carry_ref[:, 0:1]
    carry_ref[...] = jnp.broadcast_to(cT[:, tm - 1:tm], carry_ref.shape)
    chi, cmid, clo = _split3(cT * LOG2E)
    cparts = jnp.concatenate([chi, cmid, clo, jnp.ones_like(chi)], axis=0).astype(BF16)

    qaugT = _dot(selq_ref[...], cparts)
    kaugT = _dot(selk_ref[...], cparts)
    row = lax.broadcasted_iota(jnp.int32, (HEAD_DIM, tm), 0)
    vaugT = jnp.where(row == 0, 1.0, 0.0).astype(F32)

    def head_norm(t, gain):
        return t * lax.rsqrt(jnp.mean(t * t, axis=0, keepdims=True) + EPS) * gain

    for hh in range(H_FOX):
        sl = slice(hh * HEAD_DIM, (hh + 1) * HEAD_DIM)
        qn = head_norm(zT[sl], qg_ref[...])
        qTa_ref[0, hh] = jnp.concatenate([qn, qaugT[sl]], axis=0).astype(BF16)
        kn = head_norm(zT[D_FOX + hh * HEAD_DIM:D_FOX + (hh + 1) * HEAD_DIM], kg_ref[...])
        kT_ref[0, hh] = kn
        ka_ref[0, hh] = jnp.concatenate([kn, kaugT[sl]], axis=0).T.astype(BF16)
        vh = zT[2 * D_FOX + hh * HEAD_DIM:2 * D_FOX + (hh + 1) * HEAD_DIM]
        vT_ref[0, hh] = vh
        vTa_ref[0, hh] = jnp.concatenate([vh, vaugT], axis=0).astype(BF16)


def _proj_prompt(x, layer, depth, wT, w, consts, tm, carried):
    b, s, d = x.shape
    n_t = s // tm
    full = lambda shape: pl.BlockSpec(shape, lambda bi, ti: (0,) * len(shape), pipeline_mode=pl.Buffered(1))
    in_specs = [
        pl.BlockSpec((1, tm, d), lambda bi, ti: (bi, ti, 0)),
        full((1, d)),
        pl.BlockSpec((None, 3 * D_FOX + D_REST, d), lambda bi, ti: (layer, 0, 0), pipeline_mode=pl.Buffered(1)),
        full(w['wflT'].shape),
        full((8, 1)), full((HEAD_DIM, 1)), full((HEAD_DIM, 1)), full((1, D_GMLP)),
        full(consts['selq'].shape), full(consts['selk'].shape), full(consts['tri'].shape),
    ]
    operands = [x, w['gpre'], wT, w['wflT'], w['bf'], w['qg'], w['kg'], w['vgain'],
                consts['selq'], consts['selk'], consts['tri']]
    aliases = {}
    if carried is not None:
        for n, buf in enumerate(carried):
            aliases[len(operands)] = 3 + n
            in_specs.append(pl.BlockSpec(memory_space=pl.ANY))
            operands.append(buf)
    out_shape = [
        jax.ShapeDtypeStruct((b, H_FOX, AUG, s), BF16),
        jax.ShapeDtypeStruct((b, H_FOX, s, AUG), BF16),
        jax.ShapeDtypeStruct((b, H_FOX, AUG, s), BF16),
        jax.ShapeDtypeStruct((depth, b, H_FOX, HEAD_DIM, s), F32),
        jax.ShapeDtypeStruct((depth, b, H_FOX, HEAD_DIM, s), F32),
        jax.ShapeDtypeStruct((depth, b, H_FOX, s), F32),
        jax.ShapeDtypeStruct((b, s, D_REST), F32),
    ]
    out_specs = [
        pl.BlockSpec((1, H_FOX, AUG, tm), lambda bi, ti: (bi, 0, 0, ti)),
        pl.BlockSpec((1, H_FOX, tm, AUG), lambda bi, ti: (bi, 0, ti, 0)),
        pl.BlockSpec((1, H_FOX, AUG, tm), lambda bi, ti: (bi, 0, 0, ti)),
        pl.BlockSpec((None, 1, H_FOX, HEAD_DIM, tm), lambda bi, ti: (layer, bi, 0, 0, ti)),
        pl.BlockSpec((None, 1, H_FOX, HEAD_DIM, tm), lambda bi, ti: (layer, bi, 0, 0, ti)),
        pl.BlockSpec((None, 1, H_FOX, tm), lambda bi, ti: (layer, bi, 0, ti)),
        pl.BlockSpec((1, tm, D_REST), lambda bi, ti: (bi, ti, 0)),
    ]
    return pl.pallas_call(
        functools.partial(_proj_prompt_body, len(aliases)),
        grid=(b, n_t),
        in_specs=in_specs, out_specs=out_specs, out_shape=out_shape,
        scratch_shapes=[pltpu.VMEM((8, LANES), F32)],
        input_output_aliases=aliases,
        compiler_params=pltpu.CompilerParams(
            dimension_semantics=("arbitrary", "arbitrary"), vmem_limit_bytes=VMEM_LIMIT),
        name="proj_prompt",
    )(*operands)


def _fox_tile(i, j, qTa_ref, ka_ref, vTa_ref, o_ref, m_ref, acc_ref):
    bq = qTa_ref.shape[3]
    bk = ka_ref.shape[2]

    @pl.when(j == 0)
    def _():
        m_ref[...] = jnp.full_like(m_ref, NEG)
        acc_ref[...] = jnp.zeros_like(acc_ref)

    def all_heads(diagonal):
        if diagonal:
            keep = (lax.broadcasted_iota(jnp.int32, (bk, bq), 0)
                    <= lax.broadcasted_iota(jnp.int32, (bk, bq), 1))
        sTs = {h: _dot(ka_ref[0, h], qTa_ref[0, h]) for h in range(QK_AHEAD)}
        for hh in range(H_FOX):
            sT = sTs.pop(hh)
            if diagonal:
                sT = jnp.where(keep, sT, NEG)
            m_prev = m_ref[hh][0:1]
            m_new = jnp.maximum(m_prev, jnp.max(sT, axis=0, keepdims=True))
            alpha = jnp.exp2(m_prev - m_new)
            pT = jnp.exp2(sT - m_new).astype(BF16)
            if hh + QK_AHEAD < H_FOX:
                sTs[hh + QK_AHEAD] = _dot(ka_ref[0, hh + QK_AHEAD], qTa_ref[0, hh + QK_AHEAD])
            pv = _dot(vTa_ref[0, hh], pT)
            acc_ref[hh] = alpha * acc_ref[hh] + pv
            m_ref[hh] = jnp.broadcast_to(m_new, (8, bq))

    @pl.when(j < i)
    def _():
        all_heads(False)

    @pl.when(j == i)
    def _():
        all_heads(True)
        lane = lax.broadcasted_iota(jnp.int32, (bq, LANES), 1)
        for pair in range(H_FOX // 2):
            a0 = acc_ref[2 * pair]
            a1 = acc_ref[2 * pair + 1]
            o0 = (a0 * (1.0 / a0[HEAD_DIM:HEAD_DIM + 1])).T
            o1 = (a1 * (1.0 / a1[HEAD_DIM:HEAD_DIM + 1])).T
            o_ref[0, :, pair * LANES:(pair + 1) * LANES] = jnp.where(
                lane < HEAD_DIM, o0, pltpu.roll(o1, HEAD_DIM, 1))


def _mem_kv_body(mem_ref, gain_ref, wT_ref, mkT_ref, mvT_ref, mkbd_ref, mvbd_ref):
    hm = _rms(mem_ref[0], gain_ref[0]).astype(BF16)
    kvT = _dot(wT_ref[0], hm, _NT)
    mkT = kvT[:D_XATTN]
    mvT = kvT[D_XATTN:]
    mkT_ref[0, 0] = mkT
    mvT_ref[0, 0] = mvT
    mkbd_ref[...] = jnp.zeros_like(mkbd_ref)
    mvbd_ref[...] = jnp.zeros_like(mvbd_ref)
    for hh in range(H_XATTN):
        r = slice(hh * HEAD_DIM, (hh + 1) * HEAD_DIM)
        c = slice(hh * N_MEM, (hh + 1) * N_MEM)
        mkbd_ref[0, 0, r, c] = mkT[r].astype(BF16)
        mvbd_ref[0, 0, r, c] = mvT[r].astype(BF16)


def _mem_kv(mem, mem_norm, wmT):
    b, n_mem, d = mem.shape
    depth = wmT.shape[0]
    blk = lambda shape: pl.BlockSpec((1, 1) + shape, lambda l, bi: (l, bi, 0, 0))
    return pl.pallas_call(
        _mem_kv_body,
        grid=(depth, b),
        in_specs=[
            pl.BlockSpec((1, n_mem, d), lambda l, bi: (bi, 0, 0)),
            pl.BlockSpec((1, 1, d), lambda l, bi: (l, 0, 0)),
            pl.BlockSpec((1, 2 * D_XATTN, d), lambda l, bi: (l, 0, 0)),
        ],
        out_specs=[blk((D_XATTN, n_mem)), blk((D_XATTN, n_mem)),
                   blk((D_XATTN, H_XATTN * n_mem)), blk((D_XATTN, H_XATTN * n_mem))],
        out_shape=[
            jax.ShapeDtypeStruct((depth, b, D_XATTN, n_mem), F32),
            jax.ShapeDtypeStruct((depth, b, D_XATTN, n_mem), F32),
            jax.ShapeDtypeStruct((depth, b, D_XATTN, H_XATTN * n_mem), BF16),
            jax.ShapeDtypeStruct((depth, b, D_XATTN, H_XATTN * n_mem), BF16),
        ],
        compiler_params=pltpu.CompilerParams(
            dimension_semantics=("arbitrary", "arbitrary"), vmem_limit_bytes=VMEM_LIMIT),
        name="mem_kv",
    )(mem, mem_norm.reshape(depth, 1, d), wmT)


def _merge_and_project(x, fo, go, xo, gate, bn_ref, wout_ref, gpost_ref):
    merged = jnp.concatenate([
        _rms(fo, bn_ref[:, 0:D_FOX]),
        _rms(go, bn_ref[:, D_FOX:D_FOX + D_GMLP]),
        _rms(xo, bn_ref[:, D_FOX + D_GMLP:]),
    ], axis=-1) * (gate * jax.nn.sigmoid(gate))
    y = _dot(merged.astype(BF16), wout_ref[...])
    return x + _rms(y, gpost_ref[...])


def _finish_prompt_body(x_ref, o_ref, rest_ref, mkbd_ref, mvbd_ref, wsp_ref, bsp_ref, bn_ref, wout_ref,
                        gpost_ref, y_ref):
    tm = x_ref.shape[1]
    rest = rest_ref[0]
    qx = rest[:, 0:D_XATTN]
    u = rest[:, D_XATTN:D_XATTN + D_GMLP]
    vg = rest[:, D_XATTN + D_GMLP:D_XATTN + 2 * D_GMLP]
    gate = rest[:, D_XATTN + 2 * D_GMLP:]

    wi = lax.broadcasted_iota(jnp.int32, wsp_ref.shape, 0) % CHUNK
    wj = lax.broadcasted_iota(jnp.int32, wsp_ref.shape, 1)
    w_tril = jnp.where(wj <= wi, wsp_ref[...], 0.0).astype(BF16)
    grp = lax.broadcasted_iota(jnp.int32, (CHUNK, D_GMLP), 1) // HEAD_DIM
    gos = []
    for c in range(tm // CHUNK):
        rs = slice(c * CHUNK, (c + 1) * CHUNK)
        mall = _dot(w_tril, vg[rs].astype(BF16))
        mixed = mall[0:CHUNK]
        for g in range(1, G_GMLP):
            mixed = jnp.where(grp == g, mall[g * CHUNK:(g + 1) * CHUNK], mixed)
        gos.append(u[rs] * (mixed + bsp_ref[...]))
    go = jnp.concatenate(gos, axis=0)

    s = _dot(qx.astype(BF16), mkbd_ref[0, 0]) * SCALE
    ps = []
    for hh in range(H_XATTN):
        sh = s[:, hh * N_MEM:(hh + 1) * N_MEM]
        e = jnp.exp(sh - jnp.max(sh, axis=-1, keepdims=True))
        ps.append(e / jnp.sum(e, axis=-1, keepdims=True))
    p = jnp.concatenate(ps, axis=-1).astype(BF16)
    xo = _dot(p, mvbd_ref[0, 0], _NT)

    y_ref[0] = _merge_and_project(x_ref[0], o_ref[0], go, xo, gate, bn_ref, wout_ref, gpost_ref)


def _finish_prompt(x, o, rest, mkbd, mvbd, layer, w, tm):
    b, s, d = x.shape
    full = lambda shape: pl.BlockSpec(shape, lambda bi, ti: (0,) * len(shape))
    tile = lambda width: pl.BlockSpec((1, tm, width), lambda bi, ti: (bi, ti, 0))
    kvspec = pl.BlockSpec((1, 1, D_XATTN, H_XATTN * N_MEM), lambda bi, ti: (layer, bi, 0, 0))
    return pl.pallas_call(
        _finish_prompt_body,
        grid=(b, s // tm),
        in_specs=[tile(d), tile(D_FOX), tile(D_REST), kvspec, kvspec,
                  full((G_GMLP * CHUNK, CHUNK)), full((CHUNK, D_GMLP)), full((1, D_MIX)),
                  full((D_MIX, d)), full((1, d))],
        out_specs=tile(d),
        out_shape=jax.ShapeDtypeStruct((b, s, d), F32),
        compiler_params=pltpu.CompilerParams(
            dimension_semantics=("arbitrary", "arbitrary"), vmem_limit_bytes=VMEM_LIMIT),
        name="finish_prompt",
    )(x, o, rest, mkbd, mvbd, w['wsp'], w['bsp'], w['bn'], w['wout'], w['gpost'])


def _proj_sample_body(x_ref, gpre_ref, wT_ref, wflT_ref, bf_ref, qg_ref, kgrow_ref, vgain_ref, gsum_ref,
                      q_ref, k_ref, v_ref, lfT_ref, rest_ref):
    h = _rms(x_ref[...], gpre_ref[...]).astype(BF16)
    z = _dot(h, wT_ref[...], _NT)

    def head_norm(t, gain):
        t2 = t * t
        t2h = t2.astype(BF16)
        t2l = (t2 - t2h.astype(F32)).astype(BF16)
        msq = (_dot(t2h, gsum_ref[...]) + _dot(t2l, gsum_ref[...])) * (1.0 / HEAD_DIM)
        return t * lax.rsqrt(msq + EPS) * gain

    q_ref[...] = head_norm(z[:, 0:D_FOX], qg_ref[...]).astype(BF16)
    k_ref[...] = head_norm(z[:, D_FOX:2 * D_FOX], kgrow_ref[...])
    v_ref[...] = z[:, 2 * D_FOX:3 * D_FOX]
    flT = _dot(wflT_ref[...], h, _NT)
    lfT_ref[...] = _log_sigmoid(flT[0:8] + bf_ref[...])
    base = 3 * D_FOX
    rest_ref[:, 0:D_XATTN + D_GMLP] = z[:, base:base + D_XATTN + D_GMLP]
    vg = z[:, base + D_XATTN + D_GMLP:base + D_XATTN + 2 * D_GMLP]
    rest_ref[:, D_XATTN + D_GMLP:D_XATTN + 2 * D_GMLP] = _rms(vg, vgain_ref[...])
    rest_ref[:, D_XATTN + 2 * D_GMLP:] = z[:, base + D_XATTN + 2 * D_GMLP:]


def _proj_sample(x, layer, wT, w, consts):
    n, d = x.shape
    full = lambda shape: pl.BlockSpec(shape, lambda i: (0,) * len(shape))
    return pl.pallas_call(
        _proj_sample_body,
        grid=(1,),
        in_specs=[full((n, d)), full((1, d)),
                  pl.BlockSpec((None, 3 * D_FOX + D_REST, d), lambda i: (layer, 0, 0)),
                  full(w['wflT'].shape), full((8, 1)),
                  full((1, D_FOX)), full((1, D_FOX)), full((1, D_GMLP)), full(consts['gsum'].shape)],
        out_specs=[full((n, D_FOX)), full((n, D_FOX)), full((n, D_FOX)), full((8, n)), full((n, D_REST))],
        out_shape=[
            jax.ShapeDtypeStruct((n, D_FOX), BF16),
            jax.ShapeDtypeStruct((n, D_FOX), F32),
            jax.ShapeDtypeStruct((n, D_FOX), F32),
            jax.ShapeDtypeStruct((8, n), F32),
            jax.ShapeDtypeStruct((n, D_REST), F32),
        ],
        compiler_params=pltpu.CompilerParams(
            dimension_semantics=("arbitrary",), vmem_limit_bytes=VMEM_LIMIT),
        name="proj_sample",
    )(x, w['gpre'], wT, w['wflT'], w['bf'], w['qgrow'], w['kgrow'], w['vgain'], consts['gsum'])


def _expand_rows(x4, width):
    t = x4.shape[0]
    rep = jnp.concatenate([jnp.broadcast_to(x4[i:i + 1], (8, width)) for i in range(t)], axis=0)
    rowh = lax.broadcasted_iota(jnp.int32, (8 * t, width), 0) % 8
    colh = lax.broadcasted_iota(jnp.int32, (8 * t, width), 1) // HEAD_DIM
    return jnp.where(rowh == colh, rep, jnp.zeros_like(rep))


def _collapse_rows(x, t):
    width = x.shape[1]
    rowh = lax.broadcasted_iota(jnp.int32, x.shape, 0) % 8
    colh = lax.broadcasted_iota(jnp.int32, x.shape, 1) // HEAD_DIM
    xm = jnp.where(rowh == colh, x, 0.0)
    return jnp.concatenate([jnp.sum(xm[i * 8:(i + 1) * 8], axis=0, keepdims=True) for i in range(t)], axis=0)


def _decode_row(b, t_new, q_ref, knew_ref, vnew_ref, lfT_ref, rest_ref, mkT_ref, mvT_ref, tri_ref,
                k_refs, v_refs, lf_refs, fo_ref, xo_ref):
    n_pages = len(k_refs)
    nrow = 8 * t_new
    n_tok = lfT_ref.shape[1]

    qbd = _expand_rows(q_ref[...].astype(F32), D_FOX).astype(BF16)

    lf_all = jnp.concatenate([lf_refs[p][...] for p in range(n_pages)], axis=0)
    hi, mid, lo = _split3(lf_all)
    cs = _dot(jnp.concatenate([hi, mid, lo], axis=0).astype(BF16), tri_ref[...])
    np8 = 8 * n_pages
    c_in = cs[0:np8] + cs[np8:2 * np8] + cs[2 * np8:3 * np8]
    carry = jnp.zeros((8, 1), F32)
    cks = []
    for p in range(n_pages):
        cks.append(c_in[8 * p:8 * p + 8] + carry)
        carry = carry + c_in[8 * p:8 * p + 8, PAGE_SIZE - 1:PAGE_SIZE]
    jj = lax.broadcasted_iota(jnp.int32, (n_tok, LANES), 0)
    tt = lax.broadcasted_iota(jnp.int32, (n_tok, LANES), 1)
    sel = ((jj >= b * t_new) & (jj <= b * t_new + tt) & (tt < t_new)).astype(BF16)
    cn = _cumsum_lanes(lfT_ref[...], sel) + carry
    cq = jnp.concatenate([cn[:, i:i + 1] for i in range(t_new)], axis=0)

    ss = [_dot(qbd, k_refs[p][...].astype(BF16)) + (cq - jnp.concatenate([cks[p]] * t_new, axis=0))
          for p in range(n_pages)]
    pad = jnp.zeros((16 - t_new, D_FOX), F32)
    knew = jnp.concatenate([knew_ref[...], pad], axis=0).astype(BF16)
    vnew = jnp.concatenate([vnew_ref[...], pad], axis=0).astype(BF16)
    sn = _dot(qbd, knew, _NT) + (cq - jnp.concatenate([cn[:, 0:16]] * t_new, axis=0))
    qt = lax.broadcasted_iota(jnp.int32, (nrow, 16), 0) // 8
    kt = lax.broadcasted_iota(jnp.int32, (nrow, 16), 1)
    sn = jnp.where(kt <= qt, sn, NEG)

    smax = ss[0]
    for s in ss[1:]:
        smax = jnp.maximum(smax, s)
    m = jnp.maximum(jnp.max(smax, axis=-1, keepdims=True), jnp.max(sn, axis=-1, keepdims=True))
    en = jnp.exp(sn - m)
    acc = _dot(en.astype(BF16), vnew)
    esum = jnp.zeros((nrow, PAGE_SIZE), F32)
    for p in range(n_pages):
        e = jnp.exp(ss[p] - m)
        esum = esum + e
        acc = acc + _dot(e.astype(BF16), v_refs[p][...].astype(BF16), _NT)
    l = jnp.sum(esum, axis=-1, keepdims=True) + jnp.sum(en, axis=-1, keepdims=True)
    fo_ref[...] = _collapse_rows(acc * (1.0 / l), t_new)

    qx = rest_ref[:, 0:D_XATTN]
    qxbd = _expand_rows(qx, D_XATTN).astype(BF16)
    sx = _dot(qxbd, mkT_ref[...].astype(BF16)) * SCALE
    ex = jnp.exp(sx - jnp.max(sx, axis=-1, keepdims=True))
    px = (ex / jnp.sum(ex, axis=-1, keepdims=True)).astype(BF16)
    xo_ref[...] = _collapse_rows(_dot(px, mvT_ref[...].astype(BF16), _NT), t_new)


def _attend_body(n_pages, t_new, rows_per_b, qi_ref, kj_ref, pt_ref, qTa_ref, ka_ref, vTa_ref,
                 q_ref, knew_ref, vnew_ref, lfT_ref, rest_ref, mkT_ref, mvT_ref, tri_ref, *refs):
    k_refs = refs[0:n_pages]
    v_refs = refs[n_pages:2 * n_pages]
    lf_refs = refs[2 * n_pages:3 * n_pages]
    o_ref, fo_ref, xo_ref, m_ref, acc_ref = refs[3 * n_pages:]
    t = pl.program_id(1)
    _fox_tile(qi_ref[t], kj_ref[t], qTa_ref, ka_ref, vTa_ref, o_ref, m_ref, acc_ref)

    @pl.when(t < rows_per_b)
    def _():
        _decode_row(pl.program_id(0) * rows_per_b + t, t_new, q_ref, knew_ref, vnew_ref, lfT_ref, rest_ref,
                    mkT_ref, mvT_ref, tri_ref, k_refs, v_refs, lf_refs, fo_ref, xo_ref)


def _attend(layer, qTa, ka, vTa, bq, page_table, q, knew, vnew, lfT, rest, kc, vc, lfc, mkc, mvc, tri_page):
    b, _, _, s = qTa.shape
    nb, t_new, _ = q.shape
    n_pages = page_table.shape[1]
    nq = s // bq
    pairs = [(i, j) for i in range(nq) for j in range(i + 1)]
    qi = jnp.array([p[0] for p in pairs], jnp.int32)
    kj = jnp.array([p[1] for p in pairs], jnp.int32)
    assert nb % b == 0 and nb // b <= len(pairs), (nb, b, len(pairs))
    rows_per_b = nb // b
    srow = lambda bi, t: bi * rows_per_b + jnp.minimum(t, rows_per_b - 1)

    row = lambda width: pl.BlockSpec((None, t_new, width), lambda bi, t, qi, kj, pt: (srow(bi, t), 0, 0))
    page = lambda rows, p: pl.BlockSpec((None, None, rows, PAGE_SIZE),
                                        lambda bi, t, qi, kj, pt: (layer, pt[srow(bi, t), p], 0, 0))
    mem = pl.BlockSpec((None, None, D_XATTN, N_MEM), lambda bi, t, qi, kj, pt: (layer, srow(bi, t), 0, 0))
    whole = lambda a: pl.BlockSpec(a.shape, lambda bi, t, qi, kj, pt: (0,) * a.ndim)
    in_specs = [
        pl.BlockSpec((1, H_FOX, AUG, bq), lambda bi, t, qi, kj, pt: (bi, 0, 0, qi[t])),
        pl.BlockSpec((1, H_FOX, bq, AUG), lambda bi, t, qi, kj, pt: (bi, 0, kj[t], 0)),
        pl.BlockSpec((1, H_FOX, AUG, bq), lambda bi, t, qi, kj, pt: (bi, 0, 0, kj[t])),
        row(D_FOX), row(D_FOX), row(D_FOX), whole(lfT), row(D_REST), mem, mem, whole(tri_page)]
    in_specs += [page(D_FOX, p) for p in range(n_pages)]
    in_specs += [page(D_FOX, p) for p in range(n_pages)]
    in_specs += [page(H_FOX, p) for p in range(n_pages)]
    return pl.pallas_call(
        functools.partial(_attend_body, n_pages, t_new, rows_per_b),
        grid_spec=pltpu.PrefetchScalarGridSpec(
            num_scalar_prefetch=3,
            grid=(b, len(pairs)),
            in_specs=in_specs,
            out_specs=[pl.BlockSpec((1, bq, D_FOX), lambda bi, t, qi, kj, pt: (bi, qi[t], 0)),
                       row(D_FOX), row(D_XATTN)],
            scratch_shapes=[pltpu.VMEM((H_FOX, 8, bq), F32), pltpu.VMEM((H_FOX, AUG, bq), F32)]),
        out_shape=[jax.ShapeDtypeStruct((b, s, D_FOX), F32),
                   jax.ShapeDtypeStruct((nb, t_new, D_FOX), F32),
                   jax.ShapeDtypeStruct((nb, t_new, D_XATTN), F32)],
        compiler_params=pltpu.CompilerParams(
            dimension_semantics=("arbitrary", "arbitrary"), vmem_limit_bytes=VMEM_LIMIT),
        name="attend",
    )(qi, kj, page_table, qTa, ka, vTa, q, knew, vnew, lfT, rest, mkc, mvc, tri_page,
      *([kc] * n_pages), *([vc] * n_pages), *([lfc] * n_pages))


def _finish_sample_body(t_new, wsp_ref, bsp_ref, x_ref, fo_ref, xo_ref, rest_ref, bn_ref, wout_ref,
                        gpost_ref, y_ref):
    n = x_ref.shape[0]
    u = rest_ref[:, D_XATTN:D_XATTN + D_GMLP]
    vg = rest_ref[:, D_XATTN + D_GMLP:D_XATTN + 2 * D_GMLP]
    gate = rest_ref[:, D_XATTN + 2 * D_GMLP:]

    pos = lax.broadcasted_iota(jnp.int32, (n, D_GMLP), 0) % t_new
    grp = lax.broadcasted_iota(jnp.int32, (n, D_GMLP), 1) // HEAD_DIM
    mixed = jnp.zeros((n, D_GMLP), F32)
    for g in range(G_GMLP):
        for i in range(t_new):
            mixed = jnp.where((grp == g) & (pos == i), bsp_ref[g, i], mixed)
    for k in range(t_new):
        coef = jnp.zeros((n, D_GMLP), F32)
        for g in range(G_GMLP):
            for i in range(k, t_new):
                coef = jnp.where((grp == g) & (pos == i), wsp_ref[g, i * t_new + i - k], coef)
        shifted = vg if k == 0 else pltpu.roll(vg, k, 0)
        mixed = mixed + coef * shifted
    go = u * mixed
    y_ref[...] = _merge_and_project(x_ref[...], fo_ref[...], go, xo_ref[...], gate, bn_ref, wout_ref,
                                    gpost_ref)


def _finish_sample(x, fo, xo, rest, w_small, b_small, w, t_new):
    n, d = x.shape
    full = lambda shape: pl.BlockSpec(shape, lambda i: (0,) * len(shape))
    smem = pl.BlockSpec(memory_space=pltpu.SMEM)
    return pl.pallas_call(
        functools.partial(_finish_sample_body, t_new),
        grid=(1,),
        in_specs=[smem, smem, full((n, d)), full((n, D_FOX)), full((n, D_XATTN)), full((n, D_REST)),
                  full((1, D_MIX)), full((D_MIX, d)), full((1, d))],
        out_specs=full((n, d)),
        out_shape=jax.ShapeDtypeStruct((n, d), F32),
        compiler_params=pltpu.CompilerParams(
            dimension_semantics=("arbitrary",), vmem_limit_bytes=VMEM_LIMIT),
        name="finish_sample",
    )(w_small, b_small, x, fo, xo, rest, w['bn'], w['wout'], w['gpost'])


def _constants(tm):
    f = jnp.arange(D_FOX)
    gsum = (f[:, None] // HEAD_DIM == f[None, :] // HEAD_DIM).astype(BF16)
    r = jnp.arange(32)
    fh, fo = f // HEAD_DIM, f % HEAD_DIM
    selq = jnp.where((fo[:, None] < 3) & (r[None, :] == fo[:, None] * 8 + fh[:, None]), 1.0,
                     jnp.where((fo[:, None] >= 3) & (fo[:, None] < 6) & (r[None, :] == 24), 1.0, 0.0)).astype(BF16)
    selk = jnp.where((fo[:, None] < 3) & (r[None, :] == 24), 1.0,
                     jnp.where((fo[:, None] >= 3) & (fo[:, None] < 6)
                               & (r[None, :] == (fo[:, None] - 3) * 8 + fh[:, None]), -1.0, 0.0)).astype(BF16)
    t = jnp.arange(tm)
    tri = (t[:, None] <= t[None, :]).astype(BF16)
    p = jnp.arange(PAGE_SIZE)
    tri_page = (p[:, None] <= p[None, :]).astype(BF16)
    return dict(gsum=gsum, selq=selq, selk=selk, tri=tri, tri_page=tri_page)


def _layer_weights(l, norm_pre, w_in, b_forget, q_norm, k_norm, gmlp_v_norm, w_spatial, b_spatial,
                   branch_norm, w_out, norm_post):
    d = w_in.shape[1]
    fl_cols = w_in[l][:, 3 * D_FOX + D_REST:]
    wflT = jnp.concatenate([fl_cols.T, jnp.zeros((8, d), F32)], axis=0).astype(BF16)
    return dict(
        gpre=norm_pre[l].reshape(1, d),
        wflT=wflT,
        bf=b_forget[l].reshape(H_FOX, 1),
        qg=(q_norm[l] * (SCALE * LOG2E)).reshape(HEAD_DIM, 1),
        qgrow=(jnp.tile(q_norm[l], H_FOX) * SCALE).reshape(1, D_FOX),
        kg=k_norm[l].reshape(HEAD_DIM, 1),
        kgrow=jnp.tile(k_norm[l], H_FOX).reshape(1, D_FOX),
        vgain=gmlp_v_norm[l].reshape(1, D_GMLP),
        wsp=w_spatial[l].reshape(G_GMLP * CHUNK, CHUNK),
        bsp=jnp.repeat(b_spatial[l].T, HEAD_DIM, axis=1),
        bn=branch_norm[l].reshape(1, D_MIX),
        wout=w_out[l].astype(BF16),
        gpost=norm_post[l].reshape(1, d),
    )


def kernel(x_prompt, x_sample, mem_prompt, cache_fox_k, cache_fox_v, cache_fox_lf, cache_mem_k, cache_mem_v,
           page_table, norm_pre, w_in, b_forget, q_norm, k_norm, gmlp_v_norm, w_spatial, b_spatial, mem_norm,
           w_mem_kv, branch_norm, w_out, norm_post):
    depth = w_in.shape[0]
    b, s, d = x_prompt.shape
    nb, t_new, _ = x_sample.shape
    n_pool = cache_fox_k.shape[1]
    tm_proj = min(TM_PROJ, s)
    tm_finish = min(TM_FINISH, s)
    bq = min(BQ, s)
    consts = _constants(tm_proj)

    kc = jnp.transpose(cache_fox_k, (0, 1, 3, 4, 2)).reshape(depth, n_pool, D_FOX, PAGE_SIZE)
    vc = jnp.transpose(cache_fox_v, (0, 1, 3, 4, 2)).reshape(depth, n_pool, D_FOX, PAGE_SIZE)
    lfc = jnp.transpose(cache_fox_lf, (0, 1, 3, 2))
    mkc = jnp.transpose(cache_mem_k, (0, 1, 3, 4, 2)).reshape(depth, nb, D_XATTN, N_MEM)
    mvc = jnp.transpose(cache_mem_v, (0, 1, 3, 4, 2)).reshape(depth, nb, D_XATTN, N_MEM)

    wmT = jnp.transpose(w_mem_kv, (0, 2, 1)).astype(BF16)
    mkT, mvT, mkbd, mvbd = _mem_kv(mem_prompt, mem_norm, wmT)

    wT = jnp.transpose(w_in, (0, 2, 1)).astype(BF16)

    xp = x_prompt
    xs = x_sample.reshape(nb * t_new, d)
    carried = None
    kss, vss, lfss, vgs = [], [], [], []
    for l in range(depth):
        w = _layer_weights(l, norm_pre, w_in, b_forget, q_norm, k_norm, gmlp_v_norm, w_spatial, b_spatial,
                           branch_norm, w_out, norm_post)
        qTa, ka, vTa, kT_all, vT_all, lfT_all, rest = _proj_prompt(xp, l, depth, wT, w, consts, tm_proj, carried)
        carried = (kT_all, vT_all, lfT_all)
        q_s, k_s, v_s, lfT_s, rest_s = _proj_sample(xs, l, wT, w, consts)
        r3 = lambda a: a.reshape(nb, t_new, a.shape[-1])
        o, fo, xo = _attend(l, qTa, ka, vTa, bq, page_table, r3(q_s), r3(k_s), r3(v_s), lfT_s, r3(rest_s),
                            kc, vc, lfc, mkc, mvc, consts['tri_page'])
        xp = _finish_prompt(xp, o, rest, mkbd, mvbd, l, w, tm_finish)
        w_small = w_spatial[l][:, :t_new, :t_new].reshape(G_GMLP, t_new * t_new)
        b_small = b_spatial[l][:, :t_new]
        xs = _finish_sample(xs, fo.reshape(nb * t_new, D_FOX), xo.reshape(nb * t_new, D_XATTN), rest_s,
                            w_small, b_small, w, t_new)
        kss.append(k_s); vss.append(v_s); lfss.append(lfT_s)
        vgs.append(rest_s[:, D_XATTN + D_GMLP:D_XATTN + 2 * D_GMLP])

    heads_last = lambda t: jnp.transpose(t, (0, 1, 4, 2, 3))
    mem_view = lambda m: jnp.transpose(m.reshape(depth, b, H_XATTN, HEAD_DIM, N_MEM), (0, 1, 4, 2, 3))
    kT_all, vT_all, lfT_all = carried
    return (
        xp,
        xs.reshape(nb, t_new, d),
        heads_last(kT_all),
        heads_last(vT_all),
        jnp.transpose(lfT_all, (0, 1, 3, 2)),
        mem_view(mkT),
        mem_view(mvT),
        jnp.stack(kss).reshape(depth, nb, t_new, H_FOX, HEAD_DIM),
        jnp.stack(vss).reshape(depth, nb, t_new, H_FOX, HEAD_DIM),
        jnp.transpose(jnp.stack(lfss), (0, 2, 1)).reshape(depth, nb, t_new, H_FOX),
        jnp.stack(vgs).reshape(depth, nb, t_new, D_GMLP),
    )
```

```python
import functools

import jax
import jax.numpy as jnp
from jax import lax
from jax.experimental import pallas as pl
from jax.experimental.pallas import tpu as pltpu

HEAD_DIM = 64
H_FOX = 8
D_FOX = H_FOX * HEAD_DIM
G_GMLP = 4
D_GMLP = G_GMLP * HEAD_DIM
H_XATTN = 4
D_XATTN = H_XATTN * HEAD_DIM
D_MIX = D_FOX + D_GMLP + D_XATTN
N_MEM = 256
CHUNK = 128
PAGE_SIZE = 128
EPS = 1e-6
NEG = -1e30
SCALE = HEAD_DIM ** -0.5
LOG2E = 1.4426950408889634
LANES = 128
AUG = 2 * HEAD_DIM
D_REST = D_XATTN + 2 * D_GMLP + D_MIX
VMEM_LIMIT = 56 * 1024 * 1024
TM_PROJ = 512
TM_FINISH = 512
BQ = 512
QK_AHEAD = 2

F32 = jnp.float32
BF16 = jnp.bfloat16

_NN = (((1,), (0,)), ((), ()))
_NT = (((1,), (1,)), ((), ()))
_TN = (((0,), (0,)), ((), ()))


def _dot(a, b, dims=_NN):
    return lax.dot_general(a, b, dims, preferred_element_type=F32)


def _split3(x):
    hi = x.astype(BF16).astype(F32)
    r = x - hi
    mid = r.astype(BF16).astype(F32)
    lo = (r - mid).astype(BF16).astype(F32)
    return hi, mid, lo


def _log_sigmoid(x):
    return jnp.minimum(x, 0.0) - jnp.log1p(jnp.exp(-jnp.abs(x)))


def _rms(x, gain):
    return x * lax.rsqrt(jnp.mean(x * x, axis=-1, keepdims=True) + EPS) * gain


def _cumsum_lanes(xT, tri):
    hi, mid, lo = _split3(xT)
    parts = jnp.concatenate([hi, mid, lo, jnp.zeros_like(hi)], axis=0).astype(BF16)
    cs = _dot(parts, tri)
    return cs[0:8] + cs[8:16] + cs[16:24]


def _proj_prompt_body(n_carried, x_ref, gpre_ref, wT_ref, wflT_ref, bf_ref, qg_ref, kg_ref, vgain_ref,
                      selq_ref, selk_ref, tri_ref, *refs):
    qTa_ref, ka_ref, vTa_ref, kT_ref, vT_ref, lfT_ref, rest_ref, carry_ref = refs[n_carried:]
    tm = x_ref.shape[1]

    @pl.when(pl.program_id(1) == 0)
    def _():
        carry_ref[...] = jnp.zeros_like(carry_ref)

    x = x_ref[0]
    h = _rms(x, gpre_ref[...]).astype(BF16)

    za = _dot(h, wT_ref[3 * D_FOX:3 * D_FOX + D_REST], _NT)
    rest_ref[0, :, 0:D_XATTN + D_GMLP] = za[:, 0:D_XATTN + D_GMLP]
    vg = za[:, D_XATTN + D_GMLP:D_XATTN + 2 * D_GMLP]
    rest_ref[0, :, D_XATTN + D_GMLP:D_XATTN + 2 * D_GMLP] = _rms(vg, vgain_ref[...])
    rest_ref[0, :, D_XATTN + 2 * D_GMLP:] = za[:, D_XATTN + 2 * D_GMLP:]

    zT = _dot(wT_ref[0:3 * D_FOX], h, _NT)
    flT = _dot(wflT_ref[...], h, _NT)
    lfT = _log_sigmoid(flT[0:8] + bf_ref[...])
    lfT_ref[0] = lfT

    cT = _cumsum_lanes(lfT, tri_ref[...]) + carry_ref[:, 0:1]
    carry_ref[...] = jnp.broadcast_to(cT[:, tm - 1:tm], carry_ref.shape)
    chi, cmid, clo = _split3(cT * LOG2E)
    cparts = jnp.concatenate([chi, cmid, clo, jnp.ones_like(chi)], axis=0).astype(BF16)

    qaugT = _dot(selq_ref[...], cparts)
    kaugT = _dot(selk_ref[...], cparts)
    row = lax.broadcasted_iota(jnp.int32, (HEAD_DIM, tm), 0)
    vaugT = jnp.where(row == 0, 1.0, 0.0).astype(F32)

    def head_norm(t, gain):
        return t * lax.rsqrt(jnp.mean(t * t, axis=0, keepdims=True) + EPS) * gain

    for hh in range(H_FOX):
        sl = slice(hh * HEAD_DIM, (hh + 1) * HEAD_DIM)
        qn = head_norm(zT[sl], qg_ref[...])
        qTa_ref[0, hh] = jnp.concatenate([qn, qaugT[sl]], axis=0).astype(BF16)
        kn = head_norm(zT[D_FOX + hh * HEAD_DIM:D_FOX + (hh + 1) * HEAD_DIM], kg_ref[...])
        kT_ref[0, hh] = kn
        ka_ref[0, hh] = jnp.concatenate([kn, kaugT[sl]], axis=0).T.astype(BF16)
        vh = zT[2 * D_FOX + hh * HEAD_DIM:2 * D_FOX + (hh + 1) * HEAD_DIM]
        vT_ref[0, hh] = vh
        vTa_ref[0, hh] = jnp.concatenate([vh, vaugT], axis=0).astype(BF16)


def _proj_prompt(x, layer, depth, wT, w, consts, tm, carried):
    b, s, d = x.shape
    n_t = s // tm
    full = lambda shape: pl.BlockSpec(shape, lambda bi, ti: (0,) * len(shape), pipeline_mode=pl.Buffered(1))
    in_specs = [
        pl.BlockSpec((1, tm, d), lambda bi, ti: (bi, ti, 0)),
        full((1, d)),
        pl.BlockSpec((None, 3 * D_FOX + D_REST, d), lambda bi, ti: (layer, 0, 0), pipeline_mode=pl.Buffered(1)),
        full(w['wflT'].shape),
        full((8, 1)), full((HEAD_DIM, 1)), full((HEAD_DIM, 1)), full((1, D_GMLP)),
        full(consts['selq'].shape), full(consts['selk'].shape), full(consts['tri'].shape),
    ]
    operands = [x, w['gpre'], wT, w['wflT'], w['bf'], w['qg'], w['kg'], w['vgain'],
                consts['selq'], consts['selk'], consts['tri']]
    aliases = {}
    for n, buf in enumerate(carried):
        aliases[len(operands)] = 3 + n
        in_specs.append(pl.BlockSpec(memory_space=pl.ANY))
        operands.append(buf)
    out_shape = [
        jax.ShapeDtypeStruct((b, H_FOX, AUG, s), BF16),
        jax.ShapeDtypeStruct((b, H_FOX, s, AUG), BF16),
        jax.ShapeDtypeStruct((b, H_FOX, AUG, s), BF16),
        jax.ShapeDtypeStruct((depth, b, H_FOX, HEAD_DIM, s), F32),
        jax.ShapeDtypeStruct((depth, b, H_FOX, HEAD_DIM, s), F32),
        jax.ShapeDtypeStruct((depth, b, H_FOX, s), F32),
        jax.ShapeDtypeStruct((b, s, D_REST), F32),
    ]
    out_specs = [
        pl.BlockSpec((1, H_FOX, AUG, tm), lambda bi, ti: (bi, 0, 0, ti)),
        pl.BlockSpec((1, H_FOX, tm, AUG), lambda bi, ti: (bi, 0, ti, 0)),
        pl.BlockSpec((1, H_FOX, AUG, tm), lambda bi, ti: (bi, 0, 0, ti)),
        pl.BlockSpec((None, 1, H_FOX, HEAD_DIM, tm), lambda bi, ti: (layer, bi, 0, 0, ti)),
        pl.BlockSpec((None, 1, H_FOX, HEAD_DIM, tm), lambda bi, ti: (layer, bi, 0, 0, ti)),
        pl.BlockSpec((None, 1, H_FOX, tm), lambda bi, ti: (layer, bi, 0, ti)),
        pl.BlockSpec((1, tm, D_REST), lambda bi, ti: (bi, ti, 0)),
    ]
    return pl.pallas_call(
        functools.partial(_proj_prompt_body, len(aliases)),
        grid=(b, n_t),
        in_specs=in_specs, out_specs=out_specs, out_shape=out_shape,
        scratch_shapes=[pltpu.VMEM((8, LANES), F32)],
        input_output_aliases=aliases,
        compiler_params=pltpu.CompilerParams(
            dimension_semantics=("arbitrary", "arbitrary"), vmem_limit_bytes=VMEM_LIMIT),
        name="proj_prompt",
    )(*operands)


def _fox_tile(i, j, qTa_ref, ka_ref, vTa_ref, o_ref, m_ref, acc_ref):
    bq = qTa_ref.shape[3]
    bk = ka_ref.shape[2]

    @pl.when(j == 0)
    def _():
        m_ref[...] = jnp.full_like(m_ref, NEG)
        acc_ref[...] = jnp.zeros_like(acc_ref)

    def all_heads(diagonal):
        if diagonal:
            keep = (lax.broadcasted_iota(jnp.int32, (bk, bq), 0)
                    <= lax.broadcasted_iota(jnp.int32, (bk, bq), 1))
        sTs = {h: _dot(ka_ref[0, h], qTa_ref[0, h]) for h in range(QK_AHEAD)}
        for hh in range(H_FOX):
            sT = sTs.pop(hh)
            if diagonal:
                sT = jnp.where(keep, sT, NEG)
            m_prev = m_ref[hh][0:1]
            m_new = jnp.maximum(m_prev, jnp.max(sT, axis=0, keepdims=True))
            alpha = jnp.exp2(m_prev - m_new)
            pT = jnp.exp2(sT - m_new).astype(BF16)
            if hh + QK_AHEAD < H_FOX:
                sTs[hh + QK_AHEAD] = _dot(ka_ref[0, hh + QK_AHEAD], qTa_ref[0, hh + QK_AHEAD])
            pv = _dot(vTa_ref[0, hh], pT)
            acc_ref[hh] = alpha * acc_ref[hh] + pv
            m_ref[hh] = jnp.broadcast_to(m_new, (8, bq))

    @pl.when(j < i)
    def _():
        all_heads(False)

    @pl.when(j == i)
    def _():
        all_heads(True)
        lane = lax.broadcasted_iota(jnp.int32, (bq, LANES), 1)
        for pair in range(H_FOX // 2):
            a0 = acc_ref[2 * pair]
            a1 = acc_ref[2 * pair + 1]
            o0 = (a0 * (1.0 / a0[HEAD_DIM:HEAD_DIM + 1])).T
            o1 = (a1 * (1.0 / a1[HEAD_DIM:HEAD_DIM + 1])).T
            o_ref[0, :, pair * LANES:(pair + 1) * LANES] = jnp.where(
                lane < HEAD_DIM, o0, pltpu.roll(o1, HEAD_DIM, 1))


def _mem_kv_body(mem_ref, gain_ref, wT_ref, mkT_ref, mvT_ref, mkbd_ref, mvbd_ref):
    hm = _rms(mem_ref[0], gain_ref[0]).astype(BF16)
    kvT = _dot(wT_ref[0], hm, _NT)
    mkT = kvT[:D_XATTN]
    mvT = kvT[D_XATTN:]
    mkT_ref[0, 0] = mkT
    mvT_ref[0, 0] = mvT
    mkbd_ref[...] = jnp.zeros_like(mkbd_ref)
    mvbd_ref[...] = jnp.zeros_like(mvbd_ref)
    for hh in range(H_XATTN):
        r = slice(hh * HEAD_DIM, (hh + 1) * HEAD_DIM)
        c = slice(hh * N_MEM, (hh + 1) * N_MEM)
        mkbd_ref[0, 0, r, c] = mkT[r].astype(BF16)
        mvbd_ref[0, 0, r, c] = mvT[r].astype(BF16)


def _mem_kv(mem, mem_norm, wmT):
    b, n_mem, d = mem.shape
    depth = wmT.shape[0]
    blk = lambda shape: pl.BlockSpec((1, 1) + shape, lambda l, bi: (l, bi, 0, 0))
    return pl.pallas_call(
        _mem_kv_body,
        grid=(depth, b),
        in_specs=[
            pl.BlockSpec((1, n_mem, d), lambda l, bi: (bi, 0, 0)),
            pl.BlockSpec((1, 1, d), lambda l, bi: (l, 0, 0)),
            pl.BlockSpec((1, 2 * D_XATTN, d), lambda l, bi: (l, 0, 0)),
        ],
        out_specs=[blk((D_XATTN, n_mem)), blk((D_XATTN, n_mem)),
                   blk((D_XATTN, H_XATTN * n_mem)), blk((D_XATTN, H_XATTN * n_mem))],
        out_shape=[
            jax.ShapeDtypeStruct((depth, b, D_XATTN, n_mem), F32),
            jax.ShapeDtypeStruct((depth, b, D_XATTN, n_mem), F32),
            jax.ShapeDtypeStruct((depth, b, D_XATTN, H_XATTN * n_mem), BF16),
            jax.ShapeDtypeStruct((depth, b, D_XATTN, H_XATTN * n_mem), BF16),
        ],
        compiler_params=pltpu.CompilerParams(
            dimension_semantics=("arbitrary", "arbitrary"), vmem_limit_bytes=VMEM_LIMIT),
        name="mem_kv",
    )(mem, mem_norm.reshape(depth, 1, d), wmT)


def _merge_and_project(x, fo, go, xo, gate, bn_ref, wout_ref, gpost_ref):
    merged = jnp.concatenate([
        _rms(fo, bn_ref[:, 0:D_FOX]),
        _rms(go, bn_ref[:, D_FOX:D_FOX + D_GMLP]),
        _rms(xo, bn_ref[:, D_FOX + D_GMLP:]),
    ], axis=-1) * (gate * jax.nn.sigmoid(gate))
    y = _dot(merged.astype(BF16), wout_ref[...])
    return x + _rms(y, gpost_ref[...])


def _finish_prompt_body(x_ref, o_ref, rest_ref, mkbd_ref, mvbd_ref, wsp_ref, bsp_ref, bn_ref, wout_ref,
                        gpost_ref, y_ref):
    tm = x_ref.shape[1]
    rest = rest_ref[0]
    qx = rest[:, 0:D_XATTN]
    u = rest[:, D_XATTN:D_XATTN + D_GMLP]
    vg = rest[:, D_XATTN + D_GMLP:D_XATTN + 2 * D_GMLP]
    gate = rest[:, D_XATTN + 2 * D_GMLP:]

    wi = lax.broadcasted_iota(jnp.int32, wsp_ref.shape, 0) % CHUNK
    wj = lax.broadcasted_iota(jnp.int32, wsp_ref.shape, 1)
    w_tril = jnp.where(wj <= wi, wsp_ref[...], 0.0).astype(BF16)
    grp = lax.broadcasted_iota(jnp.int32, (CHUNK, D_GMLP), 1) // HEAD_DIM
    gos = []
    for c in range(tm // CHUNK):
        rs = slice(c * CHUNK, (c + 1) * CHUNK)
        mall = _dot(w_tril, vg[rs].astype(BF16))
        mixed = mall[0:CHUNK]
        for g in range(1, G_GMLP):
            mixed = jnp.where(grp == g, mall[g * CHUNK:(g + 1) * CHUNK], mixed)
        gos.append(u[rs] * (mixed + bsp_ref[...]))
    go = jnp.concatenate(gos, axis=0)

    s = _dot(qx.astype(BF16), mkbd_ref[0, 0]) * SCALE
    ps = []
    for hh in range(H_XATTN):
        sh = s[:, hh * N_MEM:(hh + 1) * N_MEM]
        e = jnp.exp(sh - jnp.max(sh, axis=-1, keepdims=True))
        ps.append(e / jnp.sum(e, axis=-1, keepdims=True))
    p = jnp.concatenate(ps, axis=-1).astype(BF16)
    xo = _dot(p, mvbd_ref[0, 0], _NT)

    y_ref[0] = _merge_and_project(x_ref[0], o_ref[0], go, xo, gate, bn_ref, wout_ref, gpost_ref)


def _finish_prompt(x, o, rest, mkbd, mvbd, layer, w, tm):
    b, s, d = x.shape
    full = lambda shape: pl.BlockSpec(shape, lambda bi, ti: (0,) * len(shape))
    tile = lambda width: pl.BlockSpec((1, tm, width), lambda bi, ti: (bi, ti, 0))
    kvspec = pl.BlockSpec((1, 1, D_XATTN, H_XATTN * N_MEM), lambda bi, ti: (layer, bi, 0, 0))
    return pl.pallas_call(
        _finish_prompt_body,
        grid=(b, s // tm),
        in_specs=[tile(d), tile(D_FOX), tile(D_REST), kvspec, kvspec,
                  full((G_GMLP * CHUNK, CHUNK)), full((CHUNK, D_GMLP)), full((1, D_MIX)),
                  full((D_MIX, d)), full((1, d))],
        out_specs=tile(d),
        out_shape=jax.ShapeDtypeStruct((b, s, d), F32),
        compiler_params=pltpu.CompilerParams(
            dimension_semantics=("arbitrary", "arbitrary"), vmem_limit_bytes=VMEM_LIMIT),
        name="finish_prompt",
    )(x, o, rest, mkbd, mvbd, w['wsp'], w['bsp'], w['bn'], w['wout'], w['gpost'])


def _proj_sample_body(x_ref, gpre_ref, wT_ref, wflT_ref, bf_ref, qg_ref, kgrow_ref, vgain_ref, gsum_ref,
                      q_ref, k_ref, v_ref, lfT_ref, rest_ref):
    h = _rms(x_ref[...], gpre_ref[...]).astype(BF16)
    z = _dot(h, wT_ref[...], _NT)

    def head_norm(t, gain):
        t2 = t * t
        t2h = t2.astype(BF16)
        t2l = (t2 - t2h.astype(F32)).astype(BF16)
        msq = (_dot(t2h, gsum_ref[...]) + _dot(t2l, gsum_ref[...])) * (1.0 / HEAD_DIM)
        return t * lax.rsqrt(msq + EPS) * gain

    q_ref[...] = head_norm(z[:, 0:D_FOX], qg_ref[...]).astype(BF16)
    k_ref[...] = head_norm(z[:, D_FOX:2 * D_FOX], kgrow_ref[...])
    v_ref[...] = z[:, 2 * D_FOX:3 * D_FOX]
    flT = _dot(wflT_ref[...], h, _NT)
    lfT_ref[...] = _log_sigmoid(flT[0:8] + bf_ref[...])
    base = 3 * D_FOX
    rest_ref[:, 0:D_XATTN + D_GMLP] = z[:, base:base + D_XATTN + D_GMLP]
    vg = z[:, base + D_XATTN + D_GMLP:base + D_XATTN + 2 * D_GMLP]
    rest_ref[:, D_XATTN + D_GMLP:D_XATTN + 2 * D_GMLP] = _rms(vg, vgain_ref[...])
    rest_ref[:, D_XATTN + 2 * D_GMLP:] = z[:, base + D_XATTN + 2 * D_GMLP:]


def _proj_sample(x, layer, wT, w, consts):
    n, d = x.shape
    full = lambda shape: pl.BlockSpec(shape, lambda i: (0,) * len(shape))
    return pl.pallas_call(
        _proj_sample_body,
        grid=(1,),
        in_specs=[full((n, d)), full((1, d)),
                  pl.BlockSpec((None, 3 * D_FOX + D_REST, d), lambda i: (layer, 0, 0)),
                  full(w['wflT'].shape), full((8, 1)),
                  full((1, D_FOX)), full((1, D_FOX)), full((1, D_GMLP)), full(consts['gsum'].shape)],
        out_specs=[full((n, D_FOX)), full((n, D_FOX)), full((n, D_FOX)), full((8, n)), full((n, D_REST))],
        out_shape=[
            jax.ShapeDtypeStruct((n, D_FOX), BF16),
            jax.ShapeDtypeStruct((n, D_FOX), F32),
            jax.ShapeDtypeStruct((n, D_FOX), F32),
            jax.ShapeDtypeStruct((8, n), F32),
            jax.ShapeDtypeStruct((n, D_REST), F32),
        ],
        compiler_params=pltpu.CompilerParams(
            dimension_semantics=("arbitrary",), vmem_limit_bytes=VMEM_LIMIT),
        name="proj_sample",
    )(x, w['gpre'], wT, w['wflT'], w['bf'], w['qgrow'], w['kgrow'], w['vgain'], consts['gsum'])


def _expand_rows(x4, width):
    t = x4.shape[0]
    rep = jnp.concatenate([jnp.broadcast_to(x4[i:i + 1], (8, width)) for i in range(t)], axis=0)
    rowh = lax.broadcasted_iota(jnp.int32, (8 * t, width), 0) % 8
    colh = lax.broadcasted_iota(jnp.int32, (8 * t, width), 1) // HEAD_DIM
    return jnp.where(rowh == colh, rep, jnp.zeros_like(rep))


def _collapse_rows(x, t):
    width = x.shape[1]
    rowh = lax.broadcasted_iota(jnp.int32, x.shape, 0) % 8
    colh = lax.broadcasted_iota(jnp.int32, x.shape, 1) // HEAD_DIM
    xm = jnp.where(rowh == colh, x, 0.0)
    return jnp.concatenate([jnp.sum(xm[i * 8:(i + 1) * 8], axis=0, keepdims=True) for i in range(t)], axis=0)


def _decode_row(b, t_new, q_ref, knew_ref, vnew_ref, lfT_ref, rest_ref, mkT_ref, mvT_ref, tri_ref,
                k_refs, v_refs, lf_refs, fo_ref, xo_ref):
    n_pages = len(k_refs)
    nrow = 8 * t_new
    n_tok = lfT_ref.shape[1]

    qbd = _expand_rows(q_ref[...].astype(F32), D_FOX).astype(BF16)

    lf_all = jnp.concatenate([lf_refs[p][...] for p in range(n_pages)], axis=0)
    hi, mid, lo = _split3(lf_all)
    cs = _dot(jnp.concatenate([hi, mid, lo], axis=0).astype(BF16), tri_ref[...])
    np8 = 8 * n_pages
    c_in = cs[0:np8] + cs[np8:2 * np8] + cs[2 * np8:3 * np8]
    carry = jnp.zeros((8, 1), F32)
    cks = []
    for p in range(n_pages):
        cks.append(c_in[8 * p:8 * p + 8] + carry)
        carry = carry + c_in[8 * p:8 * p + 8, PAGE_SIZE - 1:PAGE_SIZE]
    jj = lax.broadcasted_iota(jnp.int32, (n_tok, LANES), 0)
    tt = lax.broadcasted_iota(jnp.int32, (n_tok, LANES), 1)
    sel = ((jj >= b * t_new) & (jj <= b * t_new + tt) & (tt < t_new)).astype(BF16)
    cn = _cumsum_lanes(lfT_ref[...], sel) + carry
    cq = jnp.concatenate([cn[:, i:i + 1] for i in range(t_new)], axis=0)

    ss = [_dot(qbd, k_refs[p][...].astype(BF16)) + (cq - jnp.concatenate([cks[p]] * t_new, axis=0))
          for p in range(n_pages)]
    pad = jnp.zeros((16 - t_new, D_FOX), F32)
    knew = jnp.concatenate([knew_ref[...], pad], axis=0).astype(BF16)
    vnew = jnp.concatenate([vnew_ref[...], pad], axis=0).astype(BF16)
    sn = _dot(qbd, knew, _NT) + (cq - jnp.concatenate([cn[:, 0:16]] * t_new, axis=0))
    qt = lax.broadcasted_iota(jnp.int32, (nrow, 16), 0) // 8
    kt = lax.broadcasted_iota(jnp.int32, (nrow, 16), 1)
    sn = jnp.where(kt <= qt, sn, NEG)

    smax = ss[0]
    for s in ss[1:]:
        smax = jnp.maximum(smax, s)
    m = jnp.maximum(jnp.max(smax, axis=-1, keepdims=True), jnp.max(sn, axis=-1, keepdims=True))
    en = jnp.exp(sn - m)
    acc = _dot(en.astype(BF16), vnew)
    esum = jnp.zeros((nrow, PAGE_SIZE), F32)
    for p in range(n_pages):
        e = jnp.exp(ss[p] - m)
        esum = esum + e
        acc = acc + _dot(e.astype(BF16), v_refs[p][...].astype(BF16), _NT)
    l = jnp.sum(esum, axis=-1, keepdims=True) + jnp.sum(en, axis=-1, keepdims=True)
    fo_ref[...] = _collapse_rows(acc * (1.0 / l), t_new)

    qx = rest_ref[:, 0:D_XATTN]
    qxbd = _expand_rows(qx, D_XATTN).astype(BF16)
    sx = _dot(qxbd, mkT_ref[...].astype(BF16)) * SCALE
    ex = jnp.exp(sx - jnp.max(sx, axis=-1, keepdims=True))
    px = (ex / jnp.sum(ex, axis=-1, keepdims=True)).astype(BF16)
    xo_ref[...] = _collapse_rows(_dot(px, mvT_ref[...].astype(BF16), _NT), t_new)


def _attend_body(layer, n_pages, t_new, rows_per_b, n_rows, qi_ref, kj_ref, pt_ref, qTa_ref, ka_ref, vTa_ref,
                 q_ref, knew_ref, vnew_ref, lfT_ref, rest_ref, mkT_ref, mvT_ref, tri_ref, lfc_ref, kc_ref, vc_ref,
                 o_ref, fo_ref, xo_ref, m_ref, acc_ref, kbuf, vbuf, sem):
    t = pl.program_id(1)
    row = pl.program_id(0) * rows_per_b + t
    has_row = t < rows_per_b

    def page_copies(r, slot):
        copies = []
        for p in range(n_pages):
            page = pt_ref[r, p]
            copies.append(pltpu.make_async_copy(kc_ref.at[layer, page], kbuf.at[slot, p], sem.at[slot, 0]))
            copies.append(pltpu.make_async_copy(vc_ref.at[layer, page], vbuf.at[slot, p], sem.at[slot, 1]))
        return copies

    @pl.when(row == 0)
    def _():
        for c in page_copies(0, 0):
            c.start()

    @pl.when(has_row & (row + 1 < n_rows))
    def _():
        for c in page_copies(row + 1, (row + 1) % 2):
            c.start()

    _fox_tile(qi_ref[t], kj_ref[t], qTa_ref, ka_ref, vTa_ref, o_ref, m_ref, acc_ref)

    @pl.when(has_row)
    def _():
        slot = row % 2
        for c in page_copies(row, slot):
            c.wait()
        k_pages = [kbuf.at[slot, p] for p in range(n_pages)]
        v_pages = [vbuf.at[slot, p] for p in range(n_pages)]
        lf_pages = [lfc_ref.at[pt_ref[row, p]] for p in range(n_pages)]
        _decode_row(row, t_new, q_ref, knew_ref, vnew_ref, lfT_ref, rest_ref, mkT_ref, mvT_ref, tri_ref,
                    k_pages, v_pages, lf_pages, fo_ref, xo_ref)


def _attend(layer, qTa, ka, vTa, bq, page_table, q, knew, vnew, lfT, rest, kc, vc, lfc, mkc, mvc, tri_page):
    b, _, _, s = qTa.shape
    nb, t_new, _ = q.shape
    n_pages = page_table.shape[1]
    nq = s // bq
    pairs = [(i, j) for i in range(nq) for j in range(i + 1)]
    qi = jnp.array([p[0] for p in pairs], jnp.int32)
    kj = jnp.array([p[1] for p in pairs], jnp.int32)
    assert nb % b == 0 and nb // b <= len(pairs), (nb, b, len(pairs))
    rows_per_b = nb // b
    srow = lambda bi, t: bi * rows_per_b + jnp.minimum(t, rows_per_b - 1)

    row = lambda width: pl.BlockSpec((None, t_new, width), lambda bi, t, qi, kj, pt: (srow(bi, t), 0, 0))
    mem = pl.BlockSpec((None, None, D_XATTN, N_MEM), lambda bi, t, qi, kj, pt: (layer, srow(bi, t), 0, 0))
    whole = lambda a: pl.BlockSpec(a.shape, lambda bi, t, qi, kj, pt: (0,) * a.ndim)
    lf_table = pl.BlockSpec((None,) + lfc.shape[1:], lambda bi, t, qi, kj, pt: (layer, 0, 0, 0),
                            pipeline_mode=pl.Buffered(1))
    hbm = pl.BlockSpec(memory_space=pl.ANY)
    in_specs = [
        pl.BlockSpec((1, H_FOX, AUG, bq), lambda bi, t, qi, kj, pt: (bi, 0, 0, qi[t])),
        pl.BlockSpec((1, H_FOX, bq, AUG), lambda bi, t, qi, kj, pt: (bi, 0, kj[t], 0)),
        pl.BlockSpec((1, H_FOX, AUG, bq), lambda bi, t, qi, kj, pt: (bi, 0, 0, kj[t])),
        row(D_FOX), row(D_FOX), row(D_FOX), whole(lfT), row(D_REST), mem, mem, whole(tri_page),
        lf_table, hbm, hbm]
    page_buf = pltpu.VMEM((2, n_pages, D_FOX, PAGE_SIZE), F32)
    return pl.pallas_call(
        functools.partial(_attend_body, layer, n_pages, t_new, rows_per_b, nb),
        grid_spec=pltpu.PrefetchScalarGridSpec(
            num_scalar_prefetch=3,
            grid=(b, len(pairs)),
            in_specs=in_specs,
            out_specs=[pl.BlockSpec((1, bq, D_FOX), lambda bi, t, qi, kj, pt: (bi, qi[t], 0)),
                       row(D_FOX), row(D_XATTN)],
            scratch_shapes=[pltpu.VMEM((H_FOX, 8, bq), F32), pltpu.VMEM((H_FOX, AUG, bq), F32),
                            page_buf, page_buf, pltpu.SemaphoreType.DMA((2, 2))]),
        out_shape=[jax.ShapeDtypeStruct((b, s, D_FOX), F32),
                   jax.ShapeDtypeStruct((nb, t_new, D_FOX), F32),
                   jax.ShapeDtypeStruct((nb, t_new, D_XATTN), F32)],
        compiler_params=pltpu.CompilerParams(
            dimension_semantics=("arbitrary", "arbitrary"), vmem_limit_bytes=VMEM_LIMIT),
        name="attend",
    )(qi, kj, page_table, qTa, ka, vTa, q, knew, vnew, lfT, rest, mkc, mvc, tri_page, lfc, kc, vc)


def _finish_sample_body(t_new, wsp_ref, bsp_ref, x_ref, fo_ref, xo_ref, rest_ref, bn_ref, wout_ref,
                        gpost_ref, y_ref):
    n = x_ref.shape[0]
    u = rest_ref[:, D_XATTN:D_XATTN + D_GMLP]
    vg = rest_ref[:, D_XATTN + D_GMLP:D_XATTN + 2 * D_GMLP]
    gate = rest_ref[:, D_XATTN + 2 * D_GMLP:]

    pos = lax.broadcasted_iota(jnp.int32, (n, D_GMLP), 0) % t_new
    grp = lax.broadcasted_iota(jnp.int32, (n, D_GMLP), 1) // HEAD_DIM
    mixed = jnp.zeros((n, D_GMLP), F32)
    for g in range(G_GMLP):
        for i in range(t_new):
            mixed = jnp.where((grp == g) & (pos == i), bsp_ref[g, i], mixed)
    for k in range(t_new):
        coef = jnp.zeros((n, D_GMLP), F32)
        for g in range(G_GMLP):
            for i in range(k, t_new):
                coef = jnp.where((grp == g) & (pos == i), wsp_ref[g, i * t_new + i - k], coef)
        shifted = vg if k == 0 else pltpu.roll(vg, k, 0)
        mixed = mixed + coef * shifted
    go = u * mixed
    y_ref[...] = _merge_and_project(x_ref[...], fo_ref[...], go, xo_ref[...], gate, bn_ref, wout_ref,
                                    gpost_ref)


def _finish_sample(x, fo, xo, rest, w_small, b_small, w, t_new):
    n, d = x.shape
    full = lambda shape: pl.BlockSpec(shape, lambda i: (0,) * len(shape))
    smem = pl.BlockSpec(memory_space=pltpu.SMEM)
    return pl.pallas_call(
        functools.partial(_finish_sample_body, t_new),
        grid=(1,),
        in_specs=[smem, smem, full((n, d)), full((n, D_FOX)), full((n, D_XATTN)), full((n, D_REST)),
                  full((1, D_MIX)), full((D_MIX, d)), full((1, d))],
        out_specs=full((n, d)),
        out_shape=jax.ShapeDtypeStruct((n, d), F32),
        compiler_params=pltpu.CompilerParams(
            dimension_semantics=("arbitrary",), vmem_limit_bytes=VMEM_LIMIT),
        name="finish_sample",
    )(w_small, b_small, x, fo, xo, rest, w['bn'], w['wout'], w['gpost'])


def _constants(tm):
    f = jnp.arange(D_FOX)
    gsum = (f[:, None] // HEAD_DIM == f[None, :] // HEAD_DIM).astype(BF16)
    r = jnp.arange(32)
    fh, fo = f // HEAD_DIM, f % HEAD_DIM
    selq = jnp.where((fo[:, None] < 3) & (r[None, :] == fo[:, None] * 8 + fh[:, None]), 1.0,
                     jnp.where((fo[:, None] >= 3) & (fo[:, None] < 6) & (r[None, :] == 24), 1.0, 0.0)).astype(BF16)
    selk = jnp.where((fo[:, None] < 3) & (r[None, :] == 24), 1.0,
                     jnp.where((fo[:, None] >= 3) & (fo[:, None] < 6)
                               & (r[None, :] == (fo[:, None] - 3) * 8 + fh[:, None]), -1.0, 0.0)).astype(BF16)
    t = jnp.arange(tm)
    tri = (t[:, None] <= t[None, :]).astype(BF16)
    p = jnp.arange(PAGE_SIZE)
    tri_page = (p[:, None] <= p[None, :]).astype(BF16)
    return dict(gsum=gsum, selq=selq, selk=selk, tri=tri, tri_page=tri_page)


def _layer_weights(l, norm_pre, w_in, b_forget, q_norm, k_norm, gmlp_v_norm, w_spatial, b_spatial,
                   branch_norm, w_out, norm_post):
    d = w_in.shape[1]
    fl_cols = w_in[l][:, 3 * D_FOX + D_REST:]
    wflT = jnp.concatenate([fl_cols.T, jnp.zeros((8, d), F32)], axis=0).astype(BF16)
    return dict(
        gpre=norm_pre[l].reshape(1, d),
        wflT=wflT,
        bf=b_forget[l].reshape(H_FOX, 1),
        qg=(q_norm[l] * (SCALE * LOG2E)).reshape(HEAD_DIM, 1),
        qgrow=(jnp.tile(q_norm[l], H_FOX) * SCALE).reshape(1, D_FOX),
        kg=k_norm[l].reshape(HEAD_DIM, 1),
        kgrow=jnp.tile(k_norm[l], H_FOX).reshape(1, D_FOX),
        vgain=gmlp_v_norm[l].reshape(1, D_GMLP),
        wsp=w_spatial[l].reshape(G_GMLP * CHUNK, CHUNK),
        bsp=jnp.repeat(b_spatial[l].T, HEAD_DIM, axis=1),
        bn=branch_norm[l].reshape(1, D_MIX),
        wout=w_out[l].astype(BF16),
        gpost=norm_post[l].reshape(1, d),
    )


def kernel(x_prompt, x_sample, mem_prompt, cache_fox_k, cache_fox_v, cache_fox_lf, cache_mem_k, cache_mem_v,
           page_table, norm_pre, w_in, b_forget, q_norm, k_norm, gmlp_v_norm, w_spatial, b_spatial, mem_norm,
           w_mem_kv, branch_norm, w_out, norm_post):
    depth = w_in.shape[0]
    b, s, d = x_prompt.shape
    nb, t_new, _ = x_sample.shape
    n_pool = cache_fox_k.shape[1]
    tm_proj = min(TM_PROJ, s)
    tm_finish = min(TM_FINISH, s)
    bq = min(BQ, s)
    consts = _constants(tm_proj)

    kc = jnp.transpose(cache_fox_k, (0, 1, 3, 4, 2)).reshape(depth, n_pool, D_FOX, PAGE_SIZE)
    vc = jnp.transpose(cache_fox_v, (0, 1, 3, 4, 2)).reshape(depth, n_pool, D_FOX, PAGE_SIZE)
    lfc = jnp.transpose(cache_fox_lf, (0, 1, 3, 2))
    mkc = jnp.transpose(cache_mem_k, (0, 1, 3, 4, 2)).reshape(depth, nb, D_XATTN, N_MEM)
    mvc = jnp.transpose(cache_mem_v, (0, 1, 3, 4, 2)).reshape(depth, nb, D_XATTN, N_MEM)

    wmT = jnp.transpose(w_mem_kv, (0, 2, 1)).astype(BF16)
    mkT, mvT, mkbd, mvbd = _mem_kv(mem_prompt, mem_norm, wmT)

    wT = jnp.transpose(w_in, (0, 2, 1)).astype(BF16)

    xp = x_prompt
    xs = x_sample.reshape(nb * t_new, d)
    carried = (jnp.zeros((depth, b, H_FOX, HEAD_DIM, s), F32), jnp.zeros((depth, b, H_FOX, HEAD_DIM, s), F32),
               jnp.zeros((depth, b, H_FOX, s), F32))
    kss, vss, lfss, vgs = [], [], [], []
    for l in range(depth):
        w = _layer_weights(l, norm_pre, w_in, b_forget, q_norm, k_norm, gmlp_v_norm, w_spatial, b_spatial,
                           branch_norm, w_out, norm_post)
        qTa, ka, vTa, kT_all, vT_all, lfT_all, rest = _proj_prompt(xp, l, depth, wT, w, consts, tm_proj, carried)
        carried = (kT_all, vT_all, lfT_all)
        q_s, k_s, v_s, lfT_s, rest_s = _proj_sample(xs, l, wT, w, consts)
        r3 = lambda a: a.reshape(nb, t_new, a.shape[-1])
        o, fo, xo = _attend(l, qTa, ka, vTa, bq, page_table, r3(q_s), r3(k_s), r3(v_s), lfT_s, r3(rest_s),
                            kc, vc, lfc, mkc, mvc, consts['tri_page'])
        xp = _finish_prompt(xp, o, rest, mkbd, mvbd, l, w, tm_finish)
        w_small = w_spatial[l][:, :t_new, :t_new].reshape(G_GMLP, t_new * t_new)
        b_small = b_spatial[l][:, :t_new]
        xs = _finish_sample(xs, fo.reshape(nb * t_new, D_FOX), xo.reshape(nb * t_new, D_XATTN), rest_s,
                            w_small, b_small, w, t_new)
        kss.append(k_s); vss.append(v_s); lfss.append(lfT_s)
        vgs.append(rest_s[:, D_XATTN + D_GMLP:D_XATTN + 2 * D_GMLP])

    heads_last = lambda t: jnp.transpose(t, (0, 1, 4, 2, 3))
    mem_view = lambda m: jnp.transpose(m.reshape(depth, b, H_XATTN, HEAD_DIM, N_MEM), (0, 1, 4, 2, 3))
    kT_all, vT_all, lfT_all = carried
    return (
        xp,
        xs.reshape(nb, t_new, d),
        heads_last(kT_all),
        heads_last(vT_all),
        jnp.transpose(lfT_all, (0, 1, 3, 2)),
        mem_view(mkT),
        mem_view(mvT),
        jnp.stack(kss).reshape(depth, nb, t_new, H_FOX, HEAD_DIM),
        jnp.stack(vss).reshape(depth, nb, t_new, H_FOX, HEAD_DIM),
        jnp.transpose(jnp.stack(lfss), (0, 2, 1)).reshape(depth, nb, t_new, H_FOX),
        jnp.stack(vgs).reshape(depth, nb, t_new, D_GMLP),
    )
```

```python
import functools

import jax
import jax.numpy as jnp
from jax import lax
from jax.experimental import pallas as pl
from jax.experimental.pallas import tpu as pltpu

HEAD_DIM = 64
H_FOX = 8
D_FOX = H_FOX * HEAD_DIM
G_GMLP = 4
D_GMLP = G_GMLP * HEAD_DIM
H_XATTN = 4
D_XATTN = H_XATTN * HEAD_DIM
D_MIX = D_FOX + D_GMLP + D_XATTN
N_MEM = 256
CHUNK = 128
PAGE_SIZE = 128
EPS = 1e-6
NEG = -1e30
SCALE = HEAD_DIM ** -0.5
LOG2E = 1.4426950408889634
LANES = 128
AUG = 2 * HEAD_DIM
D_REST = D_XATTN + 2 * D_GMLP + D_MIX
VMEM_LIMIT = 56 * 1024 * 1024
TM_PROJ = 512
TM_FINISH = 512
BQ = 512
QK_AHEAD = 2
DECODE_PAGES_PER_STAGE = 4

F32 = jnp.float32
BF16 = jnp.bfloat16

_NN = (((1,), (0,)), ((), ()))
_NT = (((1,), (1,)), ((), ()))
_TN = (((0,), (0,)), ((), ()))


def _dot(a, b, dims=_NN):
    return lax.dot_general(a, b, dims, preferred_element_type=F32)


def _split3(x):
    hi = x.astype(BF16).astype(F32)
    r = x - hi
    mid = r.astype(BF16).astype(F32)
    lo = (r - mid).astype(BF16).astype(F32)
    return hi, mid, lo


def _log_sigmoid(x):
    return jnp.minimum(x, 0.0) - jnp.log1p(jnp.exp(-jnp.abs(x)))


def _rms(x, gain):
    return x * lax.rsqrt(jnp.mean(x * x, axis=-1, keepdims=True) + EPS) * gain


def _cumsum_lanes(xT, tri):
    hi, mid, lo = _split3(xT)
    parts = jnp.concatenate([hi, mid, lo, jnp.zeros_like(hi)], axis=0).astype(BF16)
    cs = _dot(parts, tri)
    return cs[0:8] + cs[8:16] + cs[16:24]


def _proj_prompt_body(n_carried, x_ref, gpre_ref, wT_ref, wflT_ref, bf_ref, qg_ref, kg_ref, vgain_ref,
                      selq_ref, selk_ref, tri_ref, *refs):
    qTa_ref, ka_ref, vTa_ref, kT_ref, vT_ref, lfT_ref, rest_ref, carry_ref = refs[n_carried:]
    tm = x_ref.shape[1]

    @pl.when(pl.program_id(1) == 0)
    def _():
        carry_ref[...] = jnp.zeros_like(carry_ref)

    x = x_ref[0]
    h = _rms(x, gpre_ref[...]).astype(BF16)

    za = _dot(h, wT_ref[3 * D_FOX:3 * D_FOX + D_REST], _NT)
    rest_ref[0, :, 0:D_XATTN + D_GMLP] = za[:, 0:D_XATTN + D_GMLP]
    vg = za[:, D_XATTN + D_GMLP:D_XATTN + 2 * D_GMLP]
    rest_ref[0, :, D_XATTN + D_GMLP:D_XATTN + 2 * D_GMLP] = _rms(vg, vgain_ref[...])
    rest_ref[0, :, D_XATTN + 2 * D_GMLP:] = za[:, D_XATTN + 2 * D_GMLP:]

    zT = _dot(wT_ref[0:3 * D_FOX], h, _NT)
    flT = _dot(wflT_ref[...], h, _NT)
    lfT = _log_sigmoid(flT[0:8] + bf_ref[...])
    lfT_ref[0] = lfT

    cT = _cumsum_lanes(lfT, tri_ref[...]) + carry_ref[:, 0:1]
    carry_ref[...] = jnp.broadcast_to(cT[:, tm - 1:tm], carry_ref.shape)
    chi, cmid, clo = _split3(cT * LOG2E)
    cparts = jnp.concatenate([chi, cmid, clo, jnp.ones_like(chi)], axis=0).astype(BF16)

    qaugT = _dot(selq_ref[...], cparts)
    kaugT = _dot(selk_ref[...], cparts)
    row = lax.broadcasted_iota(jnp.int32, (HEAD_DIM, tm), 0)
    vaugT = jnp.where(row == 0, 1.0, 0.0).astype(F32)

    def head_norm(t, gain):
        return t * lax.rsqrt(jnp.mean(t * t, axis=0, keepdims=True) + EPS) * gain

    for hh in range(H_FOX):
        sl = slice(hh * HEAD_DIM, (hh + 1) * HEAD_DIM)
        qn = head_norm(zT[sl], qg_ref[...])
        qTa_ref[0, hh] = jnp.concatenate([qn, qaugT[sl]], axis=0).astype(BF16)
        kn = head_norm(zT[D_FOX + hh * HEAD_DIM:D_FOX + (hh + 1) * HEAD_DIM], kg_ref[...])
        kT_ref[0, hh] = kn
        ka_ref[0, hh] = jnp.concatenate([kn, kaugT[sl]], axis=0).T.astype(BF16)
        vh = zT[2 * D_FOX + hh * HEAD_DIM:2 * D_FOX + (hh + 1) * HEAD_DIM]
        vT_ref[0, hh] = vh
        vTa_ref[0, hh] = jnp.concatenate([vh, vaugT], axis=0).astype(BF16)


def _proj_prompt(x, layer, depth, wT, w, consts, tm, carried):
    b, s, d = x.shape
    n_t = s // tm
    full = lambda shape: pl.BlockSpec(shape, lambda bi, ti: (0,) * len(shape), pipeline_mode=pl.Buffered(1))
    in_specs = [
        pl.BlockSpec((1, tm, d), lambda bi, ti: (bi, ti, 0)),
        full((1, d)),
        pl.BlockSpec((None, 3 * D_FOX + D_REST, d), lambda bi, ti: (layer, 0, 0), pipeline_mode=pl.Buffered(1)),
        full(w['wflT'].shape),
        full((8, 1)), full((HEAD_DIM, 1)), full((HEAD_DIM, 1)), full((1, D_GMLP)),
        full(consts['selq'].shape), full(consts['selk'].shape), full(consts['tri'].shape),
    ]
    operands = [x, w['gpre'], wT, w['wflT'], w['bf'], w['qg'], w['kg'], w['vgain'],
                consts['selq'], consts['selk'], consts['tri']]
    aliases = {}
    for n, buf in enumerate(carried):
        aliases[len(operands)] = 3 + n
        in_specs.append(pl.BlockSpec(memory_space=pl.ANY))
        operands.append(buf)
    out_shape = [
        jax.ShapeDtypeStruct((b, H_FOX, AUG, s), BF16),
        jax.ShapeDtypeStruct((b, H_FOX, s, AUG), BF16),
        jax.ShapeDtypeStruct((b, H_FOX, AUG, s), BF16),
        jax.ShapeDtypeStruct((depth, b, H_FOX, HEAD_DIM, s), F32),
        jax.ShapeDtypeStruct((depth, b, H_FOX, HEAD_DIM, s), F32),
        jax.ShapeDtypeStruct((depth, b, H_FOX, s), F32),
        jax.ShapeDtypeStruct((b, s, D_REST), F32),
    ]
    out_specs = [
        pl.BlockSpec((1, H_FOX, AUG, tm), lambda bi, ti: (bi, 0, 0, ti)),
        pl.BlockSpec((1, H_FOX, tm, AUG), lambda bi, ti: (bi, 0, ti, 0)),
        pl.BlockSpec((1, H_FOX, AUG, tm), lambda bi, ti: (bi, 0, 0, ti)),
        pl.BlockSpec((None, 1, H_FOX, HEAD_DIM, tm), lambda bi, ti: (layer, bi, 0, 0, ti)),
        pl.BlockSpec((None, 1, H_FOX, HEAD_DIM, tm), lambda bi, ti: (layer, bi, 0, 0, ti)),
        pl.BlockSpec((None, 1, H_FOX, tm), lambda bi, ti: (layer, bi, 0, ti)),
        pl.BlockSpec((1, tm, D_REST), lambda bi, ti: (bi, ti, 0)),
    ]
    return pl.pallas_call(
        functools.partial(_proj_prompt_body, len(aliases)),
        grid=(b, n_t),
        in_specs=in_specs, out_specs=out_specs, out_shape=out_shape,
        scratch_shapes=[pltpu.VMEM((8, LANES), F32)],
        input_output_aliases=aliases,
        compiler_params=pltpu.CompilerParams(
            dimension_semantics=("arbitrary", "arbitrary"), vmem_limit_bytes=VMEM_LIMIT),
        name="proj_prompt",
    )(*operands)


def _fox_tile(i, j, qTa_ref, ka_ref, vTa_ref, o_ref, m_ref, acc_ref, has_side, side_work):
    bq = qTa_ref.shape[3]
    bk = ka_ref.shape[2]

    @pl.when(j == 0)
    def _():
        m_ref[...] = jnp.full_like(m_ref, NEG)
        acc_ref[...] = jnp.zeros_like(acc_ref)

    def all_heads(diagonal, with_side):
        if diagonal:
            keep = (lax.broadcasted_iota(jnp.int32, (bk, bq), 0)
                    <= lax.broadcasted_iota(jnp.int32, (bk, bq), 1))
        side = side_work() if with_side else iter(())
        sTs = {h: _dot(ka_ref[0, h], qTa_ref[0, h]) for h in range(QK_AHEAD)}
        for hh in range(H_FOX):
            sT = sTs.pop(hh)
            if diagonal:
                sT = jnp.where(keep, sT, NEG)
            m_prev = m_ref[hh][0:1]
            m_new = jnp.maximum(m_prev, jnp.max(sT, axis=0, keepdims=True))
            alpha = jnp.exp2(m_prev - m_new)
            pT = jnp.exp2(sT - m_new).astype(BF16)
            if hh + QK_AHEAD < H_FOX:
                sTs[hh + QK_AHEAD] = _dot(ka_ref[0, hh + QK_AHEAD], qTa_ref[0, hh + QK_AHEAD])
            next(side, None)
            pv = _dot(vTa_ref[0, hh], pT)
            acc_ref[hh] = alpha * acc_ref[hh] + pv
            m_ref[hh] = jnp.broadcast_to(m_new, (8, bq))
        for _ in side:
            pass

    def finalize():
        lane = lax.broadcasted_iota(jnp.int32, (bq, LANES), 1)
        for pair in range(H_FOX // 2):
            a0 = acc_ref[2 * pair]
            a1 = acc_ref[2 * pair + 1]
            o0 = (a0 * (1.0 / a0[HEAD_DIM:HEAD_DIM + 1])).T
            o1 = (a1 * (1.0 / a1[HEAD_DIM:HEAD_DIM + 1])).T
            o_ref[0, :, pair * LANES:(pair + 1) * LANES] = jnp.where(
                lane < HEAD_DIM, o0, pltpu.roll(o1, HEAD_DIM, 1))

    for with_side in (False, True):
        side_now = has_side if with_side else jnp.logical_not(has_side)

        @pl.when((j < i) & side_now)
        def _():
            all_heads(False, with_side)

        @pl.when((j == i) & side_now)
        def _():
            all_heads(True, with_side)
            finalize()


def _mem_kv_body(mem_ref, gain_ref, wT_ref, mkT_ref, mvT_ref, mkbd_ref, mvbd_ref):
    hm = _rms(mem_ref[0], gain_ref[0]).astype(BF16)
    kvT = _dot(wT_ref[0], hm, _NT)
    mkT = kvT[:D_XATTN]
    mvT = kvT[D_XATTN:]
    mkT_ref[0, 0] = mkT
    mvT_ref[0, 0] = mvT
    mkbd_ref[...] = jnp.zeros_like(mkbd_ref)
    mvbd_ref[...] = jnp.zeros_like(mvbd_ref)
    for hh in range(H_XATTN):
        r = slice(hh * HEAD_DIM, (hh + 1) * HEAD_DIM)
        c = slice(hh * N_MEM, (hh + 1) * N_MEM)
        mkbd_ref[0, 0, r, c] = mkT[r].astype(BF16)
        mvbd_ref[0, 0, r, c] = mvT[r].astype(BF16)


def _mem_kv(mem, mem_norm, wmT):
    b, n_mem, d = mem.shape
    depth = wmT.shape[0]
    blk = lambda shape: pl.BlockSpec((1, 1) + shape, lambda l, bi: (l, bi, 0, 0))
    return pl.pallas_call(
        _mem_kv_body,
        grid=(depth, b),
        in_specs=[
            pl.BlockSpec((1, n_mem, d), lambda l, bi: (bi, 0, 0)),
            pl.BlockSpec((1, 1, d), lambda l, bi: (l, 0, 0)),
            pl.BlockSpec((1, 2 * D_XATTN, d), lambda l, bi: (l, 0, 0)),
        ],
        out_specs=[blk((D_XATTN, n_mem)), blk((D_XATTN, n_mem)),
                   blk((D_XATTN, H_XATTN * n_mem)), blk((D_XATTN, H_XATTN * n_mem))],
        out_shape=[
            jax.ShapeDtypeStruct((depth, b, D_XATTN, n_mem), F32),
            jax.ShapeDtypeStruct((depth, b, D_XATTN, n_mem), F32),
            jax.ShapeDtypeStruct((depth, b, D_XATTN, H_XATTN * n_mem), BF16),
            jax.ShapeDtypeStruct((depth, b, D_XATTN, H_XATTN * n_mem), BF16),
        ],
        compiler_params=pltpu.CompilerParams(
            dimension_semantics=("arbitrary", "arbitrary"), vmem_limit_bytes=VMEM_LIMIT),
        name="mem_kv",
    )(mem, mem_norm.reshape(depth, 1, d), wmT)


def _merge_and_project(x, fo, go, xo, gate, bn_ref, wout_ref, gpost_ref):
    merged = jnp.concatenate([
        _rms(fo, bn_ref[:, 0:D_FOX]),
        _rms(go, bn_ref[:, D_FOX:D_FOX + D_GMLP]),
        _rms(xo, bn_ref[:, D_FOX + D_GMLP:]),
    ], axis=-1) * (gate * jax.nn.sigmoid(gate))
    y = _dot(merged.astype(BF16), wout_ref[...])
    return x + _rms(y, gpost_ref[...])


def _finish_prompt_body(x_ref, o_ref, rest_ref, mkbd_ref, mvbd_ref, wsp_ref, bsp_ref, bn_ref, wout_ref,
                        gpost_ref, y_ref):
    tm = x_ref.shape[1]
    rest = rest_ref[0]
    qx = rest[:, 0:D_XATTN]
    u = rest[:, D_XATTN:D_XATTN + D_GMLP]
    vg = rest[:, D_XATTN + D_GMLP:D_XATTN + 2 * D_GMLP]
    gate = rest[:, D_XATTN + 2 * D_GMLP:]

    wi = lax.broadcasted_iota(jnp.int32, wsp_ref.shape, 0) % CHUNK
    wj = lax.broadcasted_iota(jnp.int32, wsp_ref.shape, 1)
    w_tril = jnp.where(wj <= wi, wsp_ref[...], 0.0).astype(BF16)
    grp = lax.broadcasted_iota(jnp.int32, (CHUNK, D_GMLP), 1) // HEAD_DIM
    gos = []
    for c in range(tm // CHUNK):
        rs = slice(c * CHUNK, (c + 1) * CHUNK)
        mall = _dot(w_tril, vg[rs].astype(BF16))
        mixed = mall[0:CHUNK]
        for g in range(1, G_GMLP):
            mixed = jnp.where(grp == g, mall[g * CHUNK:(g + 1) * CHUNK], mixed)
        gos.append(u[rs] * (mixed + bsp_ref[...]))
    go = jnp.concatenate(gos, axis=0)

    s = _dot(qx.astype(BF16), mkbd_ref[0, 0]) * SCALE
    ps = []
    for hh in range(H_XATTN):
        sh = s[:, hh * N_MEM:(hh + 1) * N_MEM]
        e = jnp.exp(sh - jnp.max(sh, axis=-1, keepdims=True))
        ps.append(e / jnp.sum(e, axis=-1, keepdims=True))
    p = jnp.concatenate(ps, axis=-1).astype(BF16)
    xo = _dot(p, mvbd_ref[0, 0], _NT)

    y_ref[0] = _merge_and_project(x_ref[0], o_ref[0], go, xo, gate, bn_ref, wout_ref, gpost_ref)


def _finish_prompt(x, o, rest, mkbd, mvbd, layer, w, tm):
    b, s, d = x.shape
    full = lambda shape: pl.BlockSpec(shape, lambda bi, ti: (0,) * len(shape))
    tile = lambda width: pl.BlockSpec((1, tm, width), lambda bi, ti: (bi, ti, 0))
    kvspec = pl.BlockSpec((1, 1, D_XATTN, H_XATTN * N_MEM), lambda bi, ti: (layer, bi, 0, 0))
    return pl.pallas_call(
        _finish_prompt_body,
        grid=(b, s // tm),
        in_specs=[tile(d), tile(D_FOX), tile(D_REST), kvspec, kvspec,
                  full((G_GMLP * CHUNK, CHUNK)), full((CHUNK, D_GMLP)), full((1, D_MIX)),
                  full((D_MIX, d)), full((1, d))],
        out_specs=tile(d),
        out_shape=jax.ShapeDtypeStruct((b, s, d), F32),
        compiler_params=pltpu.CompilerParams(
            dimension_semantics=("arbitrary", "arbitrary"), vmem_limit_bytes=VMEM_LIMIT),
        name="finish_prompt",
    )(x, o, rest, mkbd, mvbd, w['wsp'], w['bsp'], w['bn'], w['wout'], w['gpost'])


def _proj_sample_body(x_ref, gpre_ref, wT_ref, wflT_ref, bf_ref, qg_ref, kgrow_ref, vgain_ref, gsum_ref,
                      q_ref, k_ref, v_ref, lfT_ref, rest_ref):
    h = _rms(x_ref[...], gpre_ref[...]).astype(BF16)
    z = _dot(h, wT_ref[...], _NT)

    def head_norm(t, gain):
        t2 = t * t
        t2h = t2.astype(BF16)
        t2l = (t2 - t2h.astype(F32)).astype(BF16)
        msq = (_dot(t2h, gsum_ref[...]) + _dot(t2l, gsum_ref[...])) * (1.0 / HEAD_DIM)
        return t * lax.rsqrt(msq + EPS) * gain

    q_ref[...] = head_norm(z[:, 0:D_FOX], qg_ref[...]).astype(BF16)
    k_ref[...] = head_norm(z[:, D_FOX:2 * D_FOX], kgrow_ref[...])
    v_ref[...] = z[:, 2 * D_FOX:3 * D_FOX]
    flT = _dot(wflT_ref[...], h, _NT)
    lfT_ref[...] = _log_sigmoid(flT[0:8] + bf_ref[...])
    base = 3 * D_FOX
    rest_ref[:, 0:D_XATTN + D_GMLP] = z[:, base:base + D_XATTN + D_GMLP]
    vg = z[:, base + D_XATTN + D_GMLP:base + D_XATTN + 2 * D_GMLP]
    rest_ref[:, D_XATTN + D_GMLP:D_XATTN + 2 * D_GMLP] = _rms(vg, vgain_ref[...])
    rest_ref[:, D_XATTN + 2 * D_GMLP:] = z[:, base + D_XATTN + 2 * D_GMLP:]


def _proj_sample(x, layer, wT, w, consts):
    n, d = x.shape
    full = lambda shape: pl.BlockSpec(shape, lambda i: (0,) * len(shape))
    return pl.pallas_call(
        _proj_sample_body,
        grid=(1,),
        in_specs=[full((n, d)), full((1, d)),
                  pl.BlockSpec((None, 3 * D_FOX + D_REST, d), lambda i: (layer, 0, 0)),
                  full(w['wflT'].shape), full((8, 1)),
                  full((1, D_FOX)), full((1, D_FOX)), full((1, D_GMLP)), full(consts['gsum'].shape)],
        out_specs=[full((n, D_FOX)), full((n, D_FOX)), full((n, D_FOX)), full((8, n)), full((n, D_REST))],
        out_shape=[
            jax.ShapeDtypeStruct((n, D_FOX), BF16),
            jax.ShapeDtypeStruct((n, D_FOX), F32),
            jax.ShapeDtypeStruct((n, D_FOX), F32),
            jax.ShapeDtypeStruct((8, n), F32),
            jax.ShapeDtypeStruct((n, D_REST), F32),
        ],
        compiler_params=pltpu.CompilerParams(
            dimension_semantics=("arbitrary",), vmem_limit_bytes=VMEM_LIMIT),
        name="proj_sample",
    )(x, w['gpre'], wT, w['wflT'], w['bf'], w['qgrow'], w['kgrow'], w['vgain'], consts['gsum'])


def _expand_rows(x4, width):
    t = x4.shape[0]
    rep = jnp.concatenate([jnp.broadcast_to(x4[i:i + 1], (8, width)) for i in range(t)], axis=0)
    rowh = lax.broadcasted_iota(jnp.int32, (8 * t, width), 0) % 8
    colh = lax.broadcasted_iota(jnp.int32, (8 * t, width), 1) // HEAD_DIM
    return jnp.where(rowh == colh, rep, jnp.zeros_like(rep))


def _collapse_rows(x, t):
    width = x.shape[1]
    rowh = lax.broadcasted_iota(jnp.int32, x.shape, 0) % 8
    colh = lax.broadcasted_iota(jnp.int32, x.shape, 1) // HEAD_DIM
    xm = jnp.where(rowh == colh, x, 0.0)
    return jnp.concatenate([jnp.sum(xm[i * 8:(i + 1) * 8], axis=0, keepdims=True) for i in range(t)], axis=0)


def _decode_row(b, t_new, q_ref, knew_ref, vnew_ref, lfT_ref, rest_ref, mkT_ref, mvT_ref, tri_ref,
                k_refs, v_refs, lf_refs, fo_ref, xo_ref):
    n_pages = len(k_refs)
    nrow = 8 * t_new
    n_tok = lfT_ref.shape[1]

    qbd = _expand_rows(q_ref[...].astype(F32), D_FOX).astype(BF16)

    lf_all = jnp.concatenate([lf_refs[p][...] for p in range(n_pages)], axis=0)
    hi, mid, lo = _split3(lf_all)
    cs = _dot(jnp.concatenate([hi, mid, lo], axis=0).astype(BF16), tri_ref[...])
    np8 = 8 * n_pages
    c_in = cs[0:np8] + cs[np8:2 * np8] + cs[2 * np8:3 * np8]
    carry = jnp.zeros((8, 1), F32)
    cks = []
    for p in range(n_pages):
        cks.append(c_in[8 * p:8 * p + 8] + carry)
        carry = carry + c_in[8 * p:8 * p + 8, PAGE_SIZE - 1:PAGE_SIZE]
    jj = lax.broadcasted_iota(jnp.int32, (n_tok, LANES), 0)
    tt = lax.broadcasted_iota(jnp.int32, (n_tok, LANES), 1)
    sel = ((jj >= b * t_new) & (jj <= b * t_new + tt) & (tt < t_new)).astype(BF16)
    cn = _cumsum_lanes(lfT_ref[...], sel) + carry
    cq = jnp.concatenate([cn[:, i:i + 1] for i in range(t_new)], axis=0)

    ss = []
    for p in range(n_pages):
        ss.append(_dot(qbd, k_refs[p][...].astype(BF16)) + (cq - jnp.concatenate([cks[p]] * t_new, axis=0)))
        if p % DECODE_PAGES_PER_STAGE == DECODE_PAGES_PER_STAGE - 1 and p + 1 < n_pages:
            yield
    pad = jnp.zeros((16 - t_new, D_FOX), F32)
    knew = jnp.concatenate([knew_ref[...], pad], axis=0).astype(BF16)
    vnew = jnp.concatenate([vnew_ref[...], pad], axis=0).astype(BF16)
    sn = _dot(qbd, knew, _NT) + (cq - jnp.concatenate([cn[:, 0:16]] * t_new, axis=0))
    qt = lax.broadcasted_iota(jnp.int32, (nrow, 16), 0) // 8
    kt = lax.broadcasted_iota(jnp.int32, (nrow, 16), 1)
    sn = jnp.where(kt <= qt, sn, NEG)

    smax = ss[0]
    for s in ss[1:]:
        smax = jnp.maximum(smax, s)
    m = jnp.maximum(jnp.max(smax, axis=-1, keepdims=True), jnp.max(sn, axis=-1, keepdims=True))
    en = jnp.exp(sn - m)
    acc = _dot(en.astype(BF16), vnew)
    esum = jnp.zeros((nrow, PAGE_SIZE), F32)
    for p in range(n_pages):
        if p % DECODE_PAGES_PER_STAGE == 0:
            yield
        e = jnp.exp(ss[p] - m)
        esum = esum + e
        acc = acc + _dot(e.astype(BF16), v_refs[p][...].astype(BF16), _NT)
    l = jnp.sum(esum, axis=-1, keepdims=True) + jnp.sum(en, axis=-1, keepdims=True)
    fo_ref[...] = _collapse_rows(acc * (1.0 / l), t_new)

    qx = rest_ref[:, 0:D_XATTN]
    qxbd = _expand_rows(qx, D_XATTN).astype(BF16)
    sx = _dot(qxbd, mkT_ref[...].astype(BF16)) * SCALE
    ex = jnp.exp(sx - jnp.max(sx, axis=-1, keepdims=True))
    px = (ex / jnp.sum(ex, axis=-1, keepdims=True)).astype(BF16)
    xo_ref[...] = _collapse_rows(_dot(px, mvT_ref[...].astype(BF16), _NT), t_new)


def _attend_body(layer, n_pages, t_new, rows_per_b, n_rows, qi_ref, kj_ref, pt_ref, qTa_ref, ka_ref, vTa_ref,
                 q_ref, knew_ref, vnew_ref, lfT_ref, rest_ref, mkT_ref, mvT_ref, tri_ref, lfc_ref, kc_ref, vc_ref,
                 o_ref, fo_ref, xo_ref, m_ref, acc_ref, kbuf, vbuf, sem):
    t = pl.program_id(1)
    row = pl.program_id(0) * rows_per_b + t
    has_row = t < rows_per_b

    def page_copies(r, slot):
        copies = []
        for p in range(n_pages):
            page = pt_ref[r, p]
            copies.append(pltpu.make_async_copy(kc_ref.at[layer, page], kbuf.at[slot, p], sem.at[slot, 0]))
            copies.append(pltpu.make_async_copy(vc_ref.at[layer, page], vbuf.at[slot, p], sem.at[slot, 1]))
        return copies

    @pl.when(row == 0)
    def _():
        for c in page_copies(0, 0):
            c.start()

    @pl.when(has_row & (row + 1 < n_rows))
    def _():
        for c in page_copies(row + 1, (row + 1) % 2):
            c.start()

    @pl.when(has_row)
    def _():
        for c in page_copies(row, row % 2):
            c.wait()

    def sample_row():
        slot = row % 2
        k_pages = [kbuf.at[slot, p] for p in range(n_pages)]
        v_pages = [vbuf.at[slot, p] for p in range(n_pages)]
        lf_pages = [lfc_ref.at[pt_ref[row, p]] for p in range(n_pages)]
        return _decode_row(row, t_new, q_ref, knew_ref, vnew_ref, lfT_ref, rest_ref, mkT_ref, mvT_ref, tri_ref,
                           k_pages, v_pages, lf_pages, fo_ref, xo_ref)

    _fox_tile(qi_ref[t], kj_ref[t], qTa_ref, ka_ref, vTa_ref, o_ref, m_ref, acc_ref, has_row, sample_row)


def _attend(layer, qTa, ka, vTa, bq, page_table, q, knew, vnew, lfT, rest, kc, vc, lfc, mkc, mvc, tri_page):
    b, _, _, s = qTa.shape
    nb, t_new, _ = q.shape
    n_pages = page_table.shape[1]
    nq = s // bq
    pairs = [(i, j) for i in range(nq) for j in range(i + 1)]
    qi = jnp.array([p[0] for p in pairs], jnp.int32)
    kj = jnp.array([p[1] for p in pairs], jnp.int32)
    assert nb % b == 0 and nb // b <= len(pairs), (nb, b, len(pairs))
    rows_per_b = nb // b
    srow = lambda bi, t: bi * rows_per_b + jnp.minimum(t, rows_per_b - 1)

    row = lambda width: pl.BlockSpec((None, t_new, width), lambda bi, t, qi, kj, pt: (srow(bi, t), 0, 0))
    mem = pl.BlockSpec((None, None, D_XATTN, N_MEM), lambda bi, t, qi, kj, pt: (layer, srow(bi, t), 0, 0))
    whole = lambda a: pl.BlockSpec(a.shape, lambda bi, t, qi, kj, pt: (0,) * a.ndim)
    lf_table = pl.BlockSpec((None,) + lfc.shape[1:], lambda bi, t, qi, kj, pt: (layer, 0, 0, 0),
                            pipeline_mode=pl.Buffered(1))
    hbm = pl.BlockSpec(memory_space=pl.ANY)
    in_specs = [
        pl.BlockSpec((1, H_FOX, AUG, bq), lambda bi, t, qi, kj, pt: (bi, 0, 0, qi[t])),
        pl.BlockSpec((1, H_FOX, bq, AUG), lambda bi, t, qi, kj, pt: (bi, 0, kj[t], 0)),
        pl.BlockSpec((1, H_FOX, AUG, bq), lambda bi, t, qi, kj, pt: (bi, 0, 0, kj[t])),
        row(D_FOX), row(D_FOX), row(D_FOX), whole(lfT), row(D_REST), mem, mem, whole(tri_page),
        lf_table, hbm, hbm]
    page_buf = pltpu.VMEM((2, n_pages, D_FOX, PAGE_SIZE), F32)
    return pl.pallas_call(
        functools.partial(_attend_body, layer, n_pages, t_new, rows_per_b, nb),
        grid_spec=pltpu.PrefetchScalarGridSpec(
            num_scalar_prefetch=3,
            grid=(b, len(pairs)),
            in_specs=in_specs,
            out_specs=[pl.BlockSpec((1, bq, D_FOX), lambda bi, t, qi, kj, pt: (bi, qi[t], 0)),
                       row(D_FOX), row(D_XATTN)],
            scratch_shapes=[pltpu.VMEM((H_FOX, 8, bq), F32), pltpu.VMEM((H_FOX, AUG, bq), F32),
                            page_buf, page_buf, pltpu.SemaphoreType.DMA((2, 2))]),
        out_shape=[jax.ShapeDtypeStruct((b, s, D_FOX), F32),
                   jax.ShapeDtypeStruct((nb, t_new, D_FOX), F32),
                   jax.ShapeDtypeStruct((nb, t_new, D_XATTN), F32)],
        compiler_params=pltpu.CompilerParams(
            dimension_semantics=("arbitrary", "arbitrary"), vmem_limit_bytes=VMEM_LIMIT),
        name="attend",
    )(qi, kj, page_table, qTa, ka, vTa, q, knew, vnew, lfT, rest, mkc, mvc, tri_page, lfc, kc, vc)


def _finish_sample_body(t_new, wsp_ref, bsp_ref, x_ref, fo_ref, xo_ref, rest_ref, bn_ref, wout_ref,
                        gpost_ref, y_ref):
    n = x_ref.shape[0]
    u = rest_ref[:, D_XATTN:D_XATTN + D_GMLP]
    vg = rest_ref[:, D_XATTN + D_GMLP:D_XATTN + 2 * D_GMLP]
    gate = rest_ref[:, D_XATTN + 2 * D_GMLP:]

    pos = lax.broadcasted_iota(jnp.int32, (n, D_GMLP), 0) % t_new
    grp = lax.broadcasted_iota(jnp.int32, (n, D_GMLP), 1) // HEAD_DIM
    mixed = jnp.zeros((n, D_GMLP), F32)
    for g in range(G_GMLP):
        for i in range(t_new):
            mixed = jnp.where((grp == g) & (pos == i), bsp_ref[g, i], mixed)
    for k in range(t_new):
        coef = jnp.zeros((n, D_GMLP), F32)
        for g in range(G_GMLP):
            for i in range(k, t_new):
                coef = jnp.where((grp == g) & (pos == i), wsp_ref[g, i * t_new + i - k], coef)
        shifted = vg if k == 0 else pltpu.roll(vg, k, 0)
        mixed = mixed + coef * shifted
    go = u * mixed
    y_ref[...] = _merge_and_project(x_ref[...], fo_ref[...], go, xo_ref[...], gate, bn_ref, wout_ref,
                                    gpost_ref)


def _finish_sample(x, fo, xo, rest, w_small, b_small, w, t_new):
    n, d = x.shape
    full = lambda shape: pl.BlockSpec(shape, lambda i: (0,) * len(shape))
    smem = pl.BlockSpec(memory_space=pltpu.SMEM)
    return pl.pallas_call(
        functools.partial(_finish_sample_body, t_new),
        grid=(1,),
        in_specs=[smem, smem, full((n, d)), full((n, D_FOX)), full((n, D_XATTN)), full((n, D_REST)),
                  full((1, D_MIX)), full((D_MIX, d)), full((1, d))],
        out_specs=full((n, d)),
        out_shape=jax.ShapeDtypeStruct((n, d), F32),
        compiler_params=pltpu.CompilerParams(
            dimension_semantics=("arbitrary",), vmem_limit_bytes=VMEM_LIMIT),
        name="finish_sample",
    )(w_small, b_small, x, fo, xo, rest, w['bn'], w['wout'], w['gpost'])


def _constants(tm):
    f = jnp.arange(D_FOX)
    gsum = (f[:, None] // HEAD_DIM == f[None, :] // HEAD_DIM).astype(BF16)
    r = jnp.arange(32)
    fh, fo = f // HEAD_DIM, f % HEAD_DIM
    selq = jnp.where((fo[:, None] < 3) & (r[None, :] == fo[:, None] * 8 + fh[:, None]), 1.0,
                     jnp.where((fo[:, None] >= 3) & (fo[:, None] < 6) & (r[None, :] == 24), 1.0, 0.0)).astype(BF16)
    selk = jnp.where((fo[:, None] < 3) & (r[None, :] == 24), 1.0,
                     jnp.where((fo[:, None] >= 3) & (fo[:, None] < 6)
                               & (r[None, :] == (fo[:, None] - 3) * 8 + fh[:, None]), -1.0, 0.0)).astype(BF16)
    t = jnp.arange(tm)
    tri = (t[:, None] <= t[None, :]).astype(BF16)
    p = jnp.arange(PAGE_SIZE)
    tri_page = (p[:, None] <= p[None, :]).astype(BF16)
    return dict(gsum=gsum, selq=selq, selk=selk, tri=tri, tri_page=tri_page)


def _layer_weights(l, norm_pre, w_in, b_forget, q_norm, k_norm, gmlp_v_norm, w_spatial, b_spatial,
                   branch_norm, w_out, norm_post):
    d = w_in.shape[1]
    fl_cols = w_in[l][:, 3 * D_FOX + D_REST:]
    wflT = jnp.concatenate([fl_cols.T, jnp.zeros((8, d), F32)], axis=0).astype(BF16)
    return dict(
        gpre=norm_pre[l].reshape(1, d),
        wflT=wflT,
        bf=b_forget[l].reshape(H_FOX, 1),
        qg=(q_norm[l] * (SCALE * LOG2E)).reshape(HEAD_DIM, 1),
        qgrow=(jnp.tile(q_norm[l], H_FOX) * SCALE).reshape(1, D_FOX),
        kg=k_norm[l].reshape(HEAD_DIM, 1),
        kgrow=jnp.tile(k_norm[l], H_FOX).reshape(1, D_FOX),
        vgain=gmlp_v_norm[l].reshape(1, D_GMLP),
        wsp=w_spatial[l].reshape(G_GMLP * CHUNK, CHUNK),
        bsp=jnp.repeat(b_spatial[l].T, HEAD_DIM, axis=1),
        bn=branch_norm[l].reshape(1, D_MIX),
        wout=w_out[l].astype(BF16),
        gpost=norm_post[l].reshape(1, d),
    )


def kernel(x_prompt, x_sample, mem_prompt, cache_fox_k, cache_fox_v, cache_fox_lf, cache_mem_k, cache_mem_v,
           page_table, norm_pre, w_in, b_forget, q_norm, k_norm, gmlp_v_norm, w_spatial, b_spatial, mem_norm,
           w_mem_kv, branch_norm, w_out, norm_post):
    depth = w_in.shape[0]
    b, s, d = x_prompt.shape
    nb, t_new, _ = x_sample.shape
    n_pool = cache_fox_k.shape[1]
    tm_proj = min(TM_PROJ, s)
    tm_finish = min(TM_FINISH, s)
    bq = min(BQ, s)
    consts = _constants(tm_proj)

    kc = jnp.transpose(cache_fox_k, (0, 1, 3, 4, 2)).reshape(depth, n_pool, D_FOX, PAGE_SIZE)
    vc = jnp.transpose(cache_fox_v, (0, 1, 3, 4, 2)).reshape(depth, n_pool, D_FOX, PAGE_SIZE)
    lfc = jnp.transpose(cache_fox_lf, (0, 1, 3, 2))
    mkc = jnp.transpose(cache_mem_k, (0, 1, 3, 4, 2)).reshape(depth, nb, D_XATTN, N_MEM)
    mvc = jnp.transpose(cache_mem_v, (0, 1, 3, 4, 2)).reshape(depth, nb, D_XATTN, N_MEM)

    wmT = jnp.transpose(w_mem_kv, (0, 2, 1)).astype(BF16)
    mkT, mvT, mkbd, mvbd = _mem_kv(mem_prompt, mem_norm, wmT)

    wT = jnp.transpose(w_in, (0, 2, 1)).astype(BF16)

    xp = x_prompt
    xs = x_sample.reshape(nb * t_new, d)
    carried = (jnp.zeros((depth, b, H_FOX, HEAD_DIM, s), F32), jnp.zeros((depth, b, H_FOX, HEAD_DIM, s), F32),
               jnp.zeros((depth, b, H_FOX, s), F32))
    kss, vss, lfss, vgs = [], [], [], []
    for l in range(depth):
        w = _layer_weights(l, norm_pre, w_in, b_forget, q_norm, k_norm, gmlp_v_norm, w_spatial, b_spatial,
                           branch_norm, w_out, norm_post)
        qTa, ka, vTa, kT_all, vT_all, lfT_all, rest = _proj_prompt(xp, l, depth, wT, w, consts, tm_proj, carried)
        carried = (kT_all, vT_all, lfT_all)
        q_s, k_s, v_s, lfT_s, rest_s = _proj_sample(xs, l, wT, w, consts)
        r3 = lambda a: a.reshape(nb, t_new, a.shape[-1])
        o, fo, xo = _attend(l, qTa, ka, vTa, bq, page_table, r3(q_s), r3(k_s), r3(v_s), lfT_s, r3(rest_s),
                            kc, vc, lfc, mkc, mvc, consts['tri_page'])
        xp = _finish_prompt(xp, o, rest, mkbd, mvbd, l, w, tm_finish)
        w_small = w_spatial[l][:, :t_new, :t_new].reshape(G_GMLP, t_new * t_new)
        b_small = b_spatial[l][:, :t_new]
        xs = _finish_sample(xs, fo.reshape(nb * t_new, D_FOX), xo.reshape(nb * t_new, D_XATTN), rest_s,
                            w_small, b_small, w, t_new)
        kss.append(k_s); vss.append(v_s); lfss.append(lfT_s)
        vgs.append(rest_s[:, D_XATTN + D_GMLP:D_XATTN + 2 * D_GMLP])

    heads_last = lambda t: jnp.transpose(t, (0, 1, 4, 2, 3))
    mem_view = lambda m: jnp.transpose(m.reshape(depth, b, H_XATTN, HEAD_DIM, N_MEM), (0, 1, 4, 2, 3))
    kT_all, vT_all, lfT_all = carried
    return (
        xp,
        xs.reshape(nb, t_new, d),
        heads_last(kT_all),
        heads_last(vT_all),
        jnp.transpose(lfT_all, (0, 1, 3, 2)),
        mem_view(mkT),
        mem_view(mvT),
        jnp.stack(kss).reshape(depth, nb, t_new, H_FOX, HEAD_DIM),
        jnp.stack(vss).reshape(depth, nb, t_new, H_FOX, HEAD_DIM),
        jnp.transpose(jnp.stack(lfss), (0, 2, 1)).reshape(depth, nb, t_new, H_FOX),
        jnp.stack(vgs).reshape(depth, nb, t_new, D_GMLP),
    )
```

```python
import functools

import jax
import jax.numpy as jnp
from jax import lax
from jax.experimental import pallas as pl
from jax.experimental.pallas import tpu as pltpu

HEAD_DIM = 64
H_FOX = 8
D_FOX = H_FOX * HEAD_DIM
G_GMLP = 4
D_GMLP = G_GMLP * HEAD_DIM
H_XATTN = 4
D_XATTN = H_XATTN * HEAD_DIM
D_MIX = D_FOX + D_GMLP + D_XATTN
N_MEM = 256
CHUNK = 128
PAGE_SIZE = 128
EPS = 1e-6
NEG = -1e30
SCALE = HEAD_DIM ** -0.5
LOG2E = 1.4426950408889634
LANES = 128
AUG = 2 * HEAD_DIM
D_REST = D_XATTN + 2 * D_GMLP + D_MIX
VMEM_LIMIT = 56 * 1024 * 1024
TM_PROJ = 512
TM_FINISH = 512
FINISH_ROWS = 256
BQ = 512
QK_AHEAD = 3
DECODE_PAGES_PER_STAGE = 4

F32 = jnp.float32
BF16 = jnp.bfloat16

_NN = (((1,), (0,)), ((), ()))
_NT = (((1,), (1,)), ((), ()))
_TN = (((0,), (0,)), ((), ()))


def _dot(a, b, dims=_NN):
    return lax.dot_general(a, b, dims, preferred_element_type=F32)


def _split3(x):
    hi = x.astype(BF16).astype(F32)
    r = x - hi
    mid = r.astype(BF16).astype(F32)
    lo = (r - mid).astype(BF16).astype(F32)
    return hi, mid, lo


def _log_sigmoid(x):
    return jnp.minimum(x, 0.0) - jnp.log1p(jnp.exp(-jnp.abs(x)))


def _rms(x, gain):
    return x * lax.rsqrt(jnp.mean(x * x, axis=-1, keepdims=True) + EPS) * gain


def _cumsum_lanes(xT, tri):
    hi, mid, lo = _split3(xT)
    parts = jnp.concatenate([hi, mid, lo, jnp.zeros_like(hi)], axis=0).astype(BF16)
    cs = _dot(parts, tri)
    return cs[0:8] + cs[8:16] + cs[16:24]


def _proj_prompt_body(n_carried, x_ref, gpre_ref, wT_ref, wflT_ref, bf_ref, qg_ref, kg_ref, vgain_ref,
                      selq_ref, selk_ref, tri_ref, *refs):
    qTa_ref, ka_ref, vTa_ref, kT_ref, vT_ref, lfT_ref, rest_ref, carry_ref = refs[n_carried:]
    tm = x_ref.shape[1]

    @pl.when(pl.program_id(1) == 0)
    def _():
        carry_ref[...] = jnp.zeros_like(carry_ref)

    x = x_ref[0]
    h = _rms(x, gpre_ref[...]).astype(BF16)

    za = _dot(h, wT_ref[3 * D_FOX:3 * D_FOX + D_REST], _NT)
    rest_ref[0, :, 0:D_XATTN + D_GMLP] = za[:, 0:D_XATTN + D_GMLP]
    vg = za[:, D_XATTN + D_GMLP:D_XATTN + 2 * D_GMLP]
    rest_ref[0, :, D_XATTN + D_GMLP:D_XATTN + 2 * D_GMLP] = _rms(vg, vgain_ref[...])
    rest_ref[0, :, D_XATTN + 2 * D_GMLP:] = za[:, D_XATTN + 2 * D_GMLP:]

    zT = _dot(wT_ref[0:3 * D_FOX], h, _NT)
    flT = _dot(wflT_ref[...], h, _NT)
    lfT = _log_sigmoid(flT[0:8] + bf_ref[...])
    lfT_ref[0] = lfT

    cT = _cumsum_lanes(lfT, tri_ref[...]) + carry_ref[:, 0:1]
    carry_ref[...] = jnp.broadcast_to(cT[:, tm - 1:tm], carry_ref.shape)
    chi, cmid, clo = _split3(cT * LOG2E)
    cparts = jnp.concatenate([chi, cmid, clo, jnp.ones_like(chi)], axis=0).astype(BF16)

    qaugT = _dot(selq_ref[...], cparts)
    kaugT = _dot(selk_ref[...], cparts)
    row = lax.broadcasted_iota(jnp.int32, (HEAD_DIM, tm), 0)
    vaugT = jnp.where(row == 0, 1.0, 0.0).astype(F32)

    def head_norm(t, gain):
        return t * lax.rsqrt(jnp.mean(t * t, axis=0, keepdims=True) + EPS) * gain

    for hh in range(H_FOX):
        sl = slice(hh * HEAD_DIM, (hh + 1) * HEAD_DIM)
        qn = head_norm(zT[sl], qg_ref[...])
        qTa_ref[0, hh] = jnp.concatenate([qn, qaugT[sl]], axis=0).astype(BF16)
        kn = head_norm(zT[D_FOX + hh * HEAD_DIM:D_FOX + (hh + 1) * HEAD_DIM], kg_ref[...])
        kT_ref[0, hh] = kn
        ka_ref[0, hh] = jnp.concatenate([kn, kaugT[sl]], axis=0).T.astype(BF16)
        vh = zT[2 * D_FOX + hh * HEAD_DIM:2 * D_FOX + (hh + 1) * HEAD_DIM]
        vT_ref[0, hh] = vh
        vTa_ref[0, hh] = jnp.concatenate([vh, vaugT], axis=0).astype(BF16)


def _proj_prompt(x, layer, depth, wT, w, consts, tm, carried):
    b, s, d = x.shape
    n_t = s // tm
    full = lambda shape: pl.BlockSpec(shape, lambda bi, ti: (0,) * len(shape), pipeline_mode=pl.Buffered(1))
    in_specs = [
        pl.BlockSpec((1, tm, d), lambda bi, ti: (bi, ti, 0)),
        full((1, d)),
        pl.BlockSpec((None, 3 * D_FOX + D_REST, d), lambda bi, ti: (layer, 0, 0), pipeline_mode=pl.Buffered(1)),
        full(w['wflT'].shape),
        full((8, 1)), full((HEAD_DIM, 1)), full((HEAD_DIM, 1)), full((1, D_GMLP)),
        full(consts['selq'].shape), full(consts['selk'].shape), full(consts['tri'].shape),
    ]
    operands = [x, w['gpre'], wT, w['wflT'], w['bf'], w['qg'], w['kg'], w['vgain'],
                consts['selq'], consts['selk'], consts['tri']]
    aliases = {}
    for n, buf in enumerate(carried):
        aliases[len(operands)] = 3 + n
        in_specs.append(pl.BlockSpec(memory_space=pl.ANY))
        operands.append(buf)
    out_shape = [
        jax.ShapeDtypeStruct((b, H_FOX, AUG, s), BF16),
        jax.ShapeDtypeStruct((b, H_FOX, s, AUG), BF16),
        jax.ShapeDtypeStruct((b, H_FOX, AUG, s), BF16),
        jax.ShapeDtypeStruct((depth, b, H_FOX, HEAD_DIM, s), F32),
        jax.ShapeDtypeStruct((depth, b, H_FOX, HEAD_DIM, s), F32),
        jax.ShapeDtypeStruct((depth, b, H_FOX, s), F32),
        jax.ShapeDtypeStruct((b, s, D_REST), F32),
    ]
    out_specs = [
        pl.BlockSpec((1, H_FOX, AUG, tm), lambda bi, ti: (bi, 0, 0, ti)),
        pl.BlockSpec((1, H_FOX, tm, AUG), lambda bi, ti: (bi, 0, ti, 0)),
        pl.BlockSpec((1, H_FOX, AUG, tm), lambda bi, ti: (bi, 0, 0, ti)),
        pl.BlockSpec((None, 1, H_FOX, HEAD_DIM, tm), lambda bi, ti: (layer, bi, 0, 0, ti)),
        pl.BlockSpec((None, 1, H_FOX, HEAD_DIM, tm), lambda bi, ti: (layer, bi, 0, 0, ti)),
        pl.BlockSpec((None, 1, H_FOX, tm), lambda bi, ti: (layer, bi, 0, ti)),
        pl.BlockSpec((1, tm, D_REST), lambda bi, ti: (bi, ti, 0)),
    ]
    return pl.pallas_call(
        functools.partial(_proj_prompt_body, len(aliases)),
        grid=(b, n_t),
        in_specs=in_specs, out_specs=out_specs, out_shape=out_shape,
        scratch_shapes=[pltpu.VMEM((8, LANES), F32)],
        input_output_aliases=aliases,
        compiler_params=pltpu.CompilerParams(
            dimension_semantics=("arbitrary", "arbitrary"), vmem_limit_bytes=VMEM_LIMIT),
        name="proj_prompt",
    )(*operands)


def _fox_tile(i, j, qTa_ref, ka_ref, vTa_ref, o_ref, m_ref, acc_ref, has_side, side_work):
    bq = qTa_ref.shape[3]
    bk = ka_ref.shape[2]

    @pl.when(j == 0)
    def _():
        m_ref[...] = jnp.full_like(m_ref, NEG)
        acc_ref[...] = jnp.zeros_like(acc_ref)

    def all_heads(diagonal, with_side):
        if diagonal:
            keep = (lax.broadcasted_iota(jnp.int32, (bk, bq), 0)
                    <= lax.broadcasted_iota(jnp.int32, (bk, bq), 1))
        side = side_work() if with_side else iter(())
        sTs = {h: _dot(ka_ref[0, h], qTa_ref[0, h]) for h in range(QK_AHEAD)}
        for hh in range(H_FOX):
            sT = sTs.pop(hh)
            if diagonal:
                sT = jnp.where(keep, sT, NEG)
            m_prev = m_ref[hh][0:1]
            m_new = jnp.maximum(m_prev, jnp.max(sT, axis=0, keepdims=True))
            alpha = jnp.exp2(m_prev - m_new)
            pT = jnp.exp2(sT - m_new).astype(BF16)
            if hh + QK_AHEAD < H_FOX:
                sTs[hh + QK_AHEAD] = _dot(ka_ref[0, hh + QK_AHEAD], qTa_ref[0, hh + QK_AHEAD])
            next(side, None)
            pv = _dot(vTa_ref[0, hh], pT)
            acc_ref[hh] = alpha * acc_ref[hh] + pv
            m_ref[hh] = jnp.broadcast_to(m_new, (8, bq))
        for _ in side:
            pass

    def finalize():
        lane = lax.broadcasted_iota(jnp.int32, (bq, LANES), 1)
        for pair in range(H_FOX // 2):
            a0 = acc_ref[2 * pair]
            a1 = acc_ref[2 * pair + 1]
            o0 = (a0 * (1.0 / a0[HEAD_DIM:HEAD_DIM + 1])).T
            o1 = (a1 * (1.0 / a1[HEAD_DIM:HEAD_DIM + 1])).T
            o_ref[0, :, pair * LANES:(pair + 1) * LANES] = jnp.where(
                lane < HEAD_DIM, o0, pltpu.roll(o1, HEAD_DIM, 1))

    for with_side in (False, True):
        side_now = has_side if with_side else jnp.logical_not(has_side)

        @pl.when((j < i) & side_now)
        def _():
            all_heads(False, with_side)

        @pl.when((j == i) & side_now)
        def _():
            all_heads(True, with_side)
            finalize()


def _mem_kv_body(mem_ref, gain_ref, wT_ref, mkT_ref, mvT_ref, mkbd_ref, mvbd_ref):
    hm = _rms(mem_ref[0], gain_ref[0]).astype(BF16)
    kvT = _dot(wT_ref[0], hm, _NT)
    mkT = kvT[:D_XATTN]
    mvT = kvT[D_XATTN:]
    mkT_ref[0, 0] = mkT
    mvT_ref[0, 0] = mvT
    mkbd_ref[...] = jnp.zeros_like(mkbd_ref)
    mvbd_ref[...] = jnp.zeros_like(mvbd_ref)
    for hh in range(H_XATTN):
        r = slice(hh * HEAD_DIM, (hh + 1) * HEAD_DIM)
        c = slice(hh * N_MEM, (hh + 1) * N_MEM)
        mkbd_ref[0, 0, r, c] = mkT[r].astype(BF16)
        mvbd_ref[0, 0, r, c] = mvT[r].astype(BF16)


def _mem_kv(mem, mem_norm, wmT):
    b, n_mem, d = mem.shape
    depth = wmT.shape[0]
    blk = lambda shape: pl.BlockSpec((1, 1) + shape, lambda l, bi: (l, bi, 0, 0))
    return pl.pallas_call(
        _mem_kv_body,
        grid=(depth, b),
        in_specs=[
            pl.BlockSpec((1, n_mem, d), lambda l, bi: (bi, 0, 0)),
            pl.BlockSpec((1, 1, d), lambda l, bi: (l, 0, 0)),
            pl.BlockSpec((1, 2 * D_XATTN, d), lambda l, bi: (l, 0, 0)),
        ],
        out_specs=[blk((D_XATTN, n_mem)), blk((D_XATTN, n_mem)),
                   blk((D_XATTN, H_XATTN * n_mem)), blk((D_XATTN, H_XATTN * n_mem))],
        out_shape=[
            jax.ShapeDtypeStruct((depth, b, D_XATTN, n_mem), F32),
            jax.ShapeDtypeStruct((depth, b, D_XATTN, n_mem), F32),
            jax.ShapeDtypeStruct((depth, b, D_XATTN, H_XATTN * n_mem), BF16),
            jax.ShapeDtypeStruct((depth, b, D_XATTN, H_XATTN * n_mem), BF16),
        ],
        compiler_params=pltpu.CompilerParams(
            dimension_semantics=("arbitrary", "arbitrary"), vmem_limit_bytes=VMEM_LIMIT),
        name="mem_kv",
    )(mem, mem_norm.reshape(depth, 1, d), wmT)


def _merge_and_project(x, fo, go, xo, gate, bn_ref, wout_ref, gpost_ref):
    merged = jnp.concatenate([
        _rms(fo, bn_ref[:, 0:D_FOX]),
        _rms(go, bn_ref[:, D_FOX:D_FOX + D_GMLP]),
        _rms(xo, bn_ref[:, D_FOX + D_GMLP:]),
    ], axis=-1) * (gate * jax.nn.sigmoid(gate))
    y = _dot(merged.astype(BF16), wout_ref[...])
    return x + _rms(y, gpost_ref[...])


def _finish_rows(rows, w_tril, grp, x_ref, o_ref, rest_ref, mkbd_ref, mvbd_ref, bsp_ref, bn_ref, wout_ref,
                 gpost_ref, y_ref):
    n = rows.stop - rows.start
    rest = rest_ref[0, rows]
    qx = rest[:, 0:D_XATTN]
    u = rest[:, D_XATTN:D_XATTN + D_GMLP]
    vg = rest[:, D_XATTN + D_GMLP:D_XATTN + 2 * D_GMLP]
    gate = rest[:, D_XATTN + 2 * D_GMLP:]
    malls = [_dot(w_tril, vg[c * CHUNK:(c + 1) * CHUNK].astype(BF16)) for c in range(n // CHUNK)]
    s = _dot(qx.astype(BF16), mkbd_ref[0, 0]) * SCALE
    yield
    gos = []
    for c, mall in enumerate(malls):
        mixed = mall[0:CHUNK]
        for g in range(1, G_GMLP):
            mixed = jnp.where(grp == g, mall[g * CHUNK:(g + 1) * CHUNK], mixed)
        gos.append(u[c * CHUNK:(c + 1) * CHUNK] * (mixed + bsp_ref[...]))
    go = jnp.concatenate(gos, axis=0)
    ps = []
    for hh in range(H_XATTN):
        sh = s[:, hh * N_MEM:(hh + 1) * N_MEM]
        e = jnp.exp(sh - jnp.max(sh, axis=-1, keepdims=True))
        ps.append(e / jnp.sum(e, axis=-1, keepdims=True))
    p = jnp.concatenate(ps, axis=-1).astype(BF16)
    yield
    xo = _dot(p, mvbd_ref[0, 0], _NT)
    y_ref[0, rows] = _merge_and_project(x_ref[0, rows], o_ref[0, rows], go, xo, gate, bn_ref, wout_ref, gpost_ref)


def _finish_prompt_body(x_ref, o_ref, rest_ref, mkbd_ref, mvbd_ref, wsp_ref, bsp_ref, bn_ref, wout_ref,
                        gpost_ref, y_ref):
    tm = x_ref.shape[1]
    wi = lax.broadcasted_iota(jnp.int32, wsp_ref.shape, 0) % CHUNK
    wj = lax.broadcasted_iota(jnp.int32, wsp_ref.shape, 1)
    w_tril = jnp.where(wj <= wi, wsp_ref[...], 0.0).astype(BF16)
    grp = lax.broadcasted_iota(jnp.int32, (CHUNK, D_GMLP), 1) // HEAD_DIM
    waiting = [_finish_rows(slice(r0, r0 + FINISH_ROWS), w_tril, grp, x_ref, o_ref, rest_ref, mkbd_ref, mvbd_ref,
                            bsp_ref, bn_ref, wout_ref, gpost_ref, y_ref) for r0 in range(0, tm, FINISH_ROWS)]
    live = []
    while waiting or live:
        if waiting:
            live.append(waiting.pop(0))
        for g in list(live):
            if next(g, StopIteration) is StopIteration:
                live.remove(g)


def _finish_prompt(x, o, rest, mkbd, mvbd, layer, w, tm):
    b, s, d = x.shape
    full = lambda shape: pl.BlockSpec(shape, lambda bi, ti: (0,) * len(shape))
    tile = lambda width: pl.BlockSpec((1, tm, width), lambda bi, ti: (bi, ti, 0))
    kvspec = pl.BlockSpec((1, 1, D_XATTN, H_XATTN * N_MEM), lambda bi, ti: (layer, bi, 0, 0))
    return pl.pallas_call(
        _finish_prompt_body,
        grid=(b, s // tm),
        in_specs=[tile(d), tile(D_FOX), tile(D_REST), kvspec, kvspec,
                  full((G_GMLP * CHUNK, CHUNK)), full((CHUNK, D_GMLP)), full((1, D_MIX)),
                  full((D_MIX, d)), full((1, d))],
        out_specs=tile(d),
        out_shape=jax.ShapeDtypeStruct((b, s, d), F32),
        compiler_params=pltpu.CompilerParams(
            dimension_semantics=("arbitrary", "arbitrary"), vmem_limit_bytes=VMEM_LIMIT),
        name="finish_prompt",
    )(x, o, rest, mkbd, mvbd, w['wsp'], w['bsp'], w['bn'], w['wout'], w['gpost'])


def _proj_sample_body(x_ref, gpre_ref, wT_ref, wflT_ref, bf_ref, qg_ref, kgrow_ref, vgain_ref, gsum_ref,
                      q_ref, k_ref, v_ref, lfT_ref, rest_ref):
    h = _rms(x_ref[...], gpre_ref[...]).astype(BF16)
    z = _dot(h, wT_ref[...], _NT)

    def head_norm(t, gain):
        t2 = t * t
        t2h = t2.astype(BF16)
        t2l = (t2 - t2h.astype(F32)).astype(BF16)
        msq = (_dot(t2h, gsum_ref[...]) + _dot(t2l, gsum_ref[...])) * (1.0 / HEAD_DIM)
        return t * lax.rsqrt(msq + EPS) * gain

    q_ref[...] = head_norm(z[:, 0:D_FOX], qg_ref[...]).astype(BF16)
    k_ref[...] = head_norm(z[:, D_FOX:2 * D_FOX], kgrow_ref[...])
    v_ref[...] = z[:, 2 * D_FOX:3 * D_FOX]
    flT = _dot(wflT_ref[...], h, _NT)
    lfT_ref[...] = _log_sigmoid(flT[0:8] + bf_ref[...])
    base = 3 * D_FOX
    rest_ref[:, 0:D_XATTN + D_GMLP] = z[:, base:base + D_XATTN + D_GMLP]
    vg = z[:, base + D_XATTN + D_GMLP:base + D_XATTN + 2 * D_GMLP]
    rest_ref[:, D_XATTN + D_GMLP:D_XATTN + 2 * D_GMLP] = _rms(vg, vgain_ref[...])
    rest_ref[:, D_XATTN + 2 * D_GMLP:] = z[:, base + D_XATTN + 2 * D_GMLP:]


def _proj_sample(x, layer, wT, w, consts):
    n, d = x.shape
    full = lambda shape: pl.BlockSpec(shape, lambda i: (0,) * len(shape))
    return pl.pallas_call(
        _proj_sample_body,
        grid=(1,),
        in_specs=[full((n, d)), full((1, d)),
                  pl.BlockSpec((None, 3 * D_FOX + D_REST, d), lambda i: (layer, 0, 0)),
                  full(w['wflT'].shape), full((8, 1)),
                  full((1, D_FOX)), full((1, D_FOX)), full((1, D_GMLP)), full(consts['gsum'].shape)],
        out_specs=[full((n, D_FOX)), full((n, D_FOX)), full((n, D_FOX)), full((8, n)), full((n, D_REST))],
        out_shape=[
            jax.ShapeDtypeStruct((n, D_FOX), BF16),
            jax.ShapeDtypeStruct((n, D_FOX), F32),
            jax.ShapeDtypeStruct((n, D_FOX), F32),
            jax.ShapeDtypeStruct((8, n), F32),
            jax.ShapeDtypeStruct((n, D_REST), F32),
        ],
        compiler_params=pltpu.CompilerParams(
            dimension_semantics=("arbitrary",), vmem_limit_bytes=VMEM_LIMIT),
        name="proj_sample",
    )(x, w['gpre'], wT, w['wflT'], w['bf'], w['qgrow'], w['kgrow'], w['vgain'], consts['gsum'])


def _expand_rows(x4, width):
    t = x4.shape[0]
    rep = jnp.concatenate([jnp.broadcast_to(x4[i:i + 1], (8, width)) for i in range(t)], axis=0)
    rowh = lax.broadcasted_iota(jnp.int32, (8 * t, width), 0) % 8
    colh = lax.broadcasted_iota(jnp.int32, (8 * t, width), 1) // HEAD_DIM
    return jnp.where(rowh == colh, rep, jnp.zeros_like(rep))


def _collapse_rows(x, t):
    width = x.shape[1]
    rowh = lax.broadcasted_iota(jnp.int32, x.shape, 0) % 8
    colh = lax.broadcasted_iota(jnp.int32, x.shape, 1) // HEAD_DIM
    xm = jnp.where(rowh == colh, x, 0.0)
    return jnp.concatenate([jnp.sum(xm[i * 8:(i + 1) * 8], axis=0, keepdims=True) for i in range(t)], axis=0)


def _decode_row(b, t_new, q_ref, knew_ref, vnew_ref, lfT_ref, rest_ref, mkT_ref, mvT_ref, tri_ref,
                k_refs, v_refs, lf_refs, fo_ref, xo_ref):
    n_pages = len(k_refs)
    nrow = 8 * t_new
    n_tok = lfT_ref.shape[1]

    qbd = _expand_rows(q_ref[...].astype(F32), D_FOX).astype(BF16)

    lf_all = jnp.concatenate([lf_refs[p][...] for p in range(n_pages)], axis=0)
    hi, mid, lo = _split3(lf_all)
    cs = _dot(jnp.concatenate([hi, mid, lo], axis=0).astype(BF16), tri_ref[...])
    np8 = 8 * n_pages
    c_in = cs[0:np8] + cs[np8:2 * np8] + cs[2 * np8:3 * np8]
    carry = jnp.zeros((8, 1), F32)
    cks = []
    for p in range(n_pages):
        cks.append(c_in[8 * p:8 * p + 8] + carry)
        carry = carry + c_in[8 * p:8 * p + 8, PAGE_SIZE - 1:PAGE_SIZE]
    jj = lax.broadcasted_iota(jnp.int32, (n_tok, LANES), 0)
    tt = lax.broadcasted_iota(jnp.int32, (n_tok, LANES), 1)
    sel = ((jj >= b * t_new) & (jj <= b * t_new + tt) & (tt < t_new)).astype(BF16)
    cn = _cumsum_lanes(lfT_ref[...], sel) + carry
    cq = jnp.concatenate([cn[:, i:i + 1] for i in range(t_new)], axis=0)

    ss = []
    for p in range(n_pages):
        ss.append(_dot(qbd, k_refs[p][...].astype(BF16)) + (cq - jnp.concatenate([cks[p]] * t_new, axis=0)))
        if p % DECODE_PAGES_PER_STAGE == DECODE_PAGES_PER_STAGE - 1 and p + 1 < n_pages:
            yield
    pad = jnp.zeros((16 - t_new, D_FOX), F32)
    knew = jnp.concatenate([knew_ref[...], pad], axis=0).astype(BF16)
    vnew = jnp.concatenate([vnew_ref[...], pad], axis=0).astype(BF16)
    sn = _dot(qbd, knew, _NT) + (cq - jnp.concatenate([cn[:, 0:16]] * t_new, axis=0))
    qt = lax.broadcasted_iota(jnp.int32, (nrow, 16), 0) // 8
    kt = lax.broadcasted_iota(jnp.int32, (nrow, 16), 1)
    sn = jnp.where(kt <= qt, sn, NEG)

    smax = ss[0]
    for s in ss[1:]:
        smax = jnp.maximum(smax, s)
    m = jnp.maximum(jnp.max(smax, axis=-1, keepdims=True), jnp.max(sn, axis=-1, keepdims=True))
    en = jnp.exp(sn - m)
    acc = _dot(en.astype(BF16), vnew)
    esum = jnp.zeros((nrow, PAGE_SIZE), F32)
    for p in range(n_pages):
        if p % DECODE_PAGES_PER_STAGE == 0:
            yield
        e = jnp.exp(ss[p] - m)
        esum = esum + e
        acc = acc + _dot(e.astype(BF16), v_refs[p][...].astype(BF16), _NT)
    l = jnp.sum(esum, axis=-1, keepdims=True) + jnp.sum(en, axis=-1, keepdims=True)
    fo_ref[...] = _collapse_rows(acc * (1.0 / l), t_new)

    qx = rest_ref[:, 0:D_XATTN]
    qxbd = _expand_rows(qx, D_XATTN).astype(BF16)
    sx = _dot(qxbd, mkT_ref[...].astype(BF16)) * SCALE
    ex = jnp.exp(sx - jnp.max(sx, axis=-1, keepdims=True))
    px = (ex / jnp.sum(ex, axis=-1, keepdims=True)).astype(BF16)
    xo_ref[...] = _collapse_rows(_dot(px, mvT_ref[...].astype(BF16), _NT), t_new)


def _attend_body(layer, n_pages, t_new, rows_per_b, n_rows, qi_ref, kj_ref, pt_ref, qTa_ref, ka_ref, vTa_ref,
                 q_ref, knew_ref, vnew_ref, lfT_ref, rest_ref, mkT_ref, mvT_ref, tri_ref, lfc_ref, kc_ref, vc_ref,
                 o_ref, fo_ref, xo_ref, m_ref, acc_ref, kbuf, vbuf, sem):
    t = pl.program_id(1)
    row = pl.program_id(0) * rows_per_b + t
    has_row = t < rows_per_b

    def page_copies(r, slot):
        copies = []
        for p in range(n_pages):
            page = pt_ref[r, p]
            copies.append(pltpu.make_async_copy(kc_ref.at[layer, page], kbuf.at[slot, p], sem.at[slot, 0]))
            copies.append(pltpu.make_async_copy(vc_ref.at[layer, page], vbuf.at[slot, p], sem.at[slot, 1]))
        return copies

    @pl.when(row == 0)
    def _():
        for c in page_copies(0, 0):
            c.start()

    @pl.when(has_row & (row + 1 < n_rows))
    def _():
        for c in page_copies(row + 1, (row + 1) % 2):
            c.start()

    @pl.when(has_row)
    def _():
        for c in page_copies(row, row % 2):
            c.wait()

    def sample_row():
        slot = row % 2
        k_pages = [kbuf.at[slot, p] for p in range(n_pages)]
        v_pages = [vbuf.at[slot, p] for p in range(n_pages)]
        lf_pages = [lfc_ref.at[pt_ref[row, p]] for p in range(n_pages)]
        return _decode_row(row, t_new, q_ref, knew_ref, vnew_ref, lfT_ref, rest_ref, mkT_ref, mvT_ref, tri_ref,
                           k_pages, v_pages, lf_pages, fo_ref, xo_ref)

    _fox_tile(qi_ref[t], kj_ref[t], qTa_ref, ka_ref, vTa_ref, o_ref, m_ref, acc_ref, has_row, sample_row)


def _attend(layer, qTa, ka, vTa, bq, page_table, q, knew, vnew, lfT, rest, kc, vc, lfc, mkc, mvc, tri_page):
    b, _, _, s = qTa.shape
    nb, t_new, _ = q.shape
    n_pages = page_table.shape[1]
    nq = s // bq
    pairs = [(i, j) for i in range(nq) for j in range(i + 1)]
    qi = jnp.array([p[0] for p in pairs], jnp.int32)
    kj = jnp.array([p[1] for p in pairs], jnp.int32)
    assert nb % b == 0 and nb // b <= len(pairs), (nb, b, len(pairs))
    rows_per_b = nb // b
    srow = lambda bi, t: bi * rows_per_b + jnp.minimum(t, rows_per_b - 1)

    row = lambda width: pl.BlockSpec((None, t_new, width), lambda bi, t, qi, kj, pt: (srow(bi, t), 0, 0))
    mem = pl.BlockSpec((None, None, D_XATTN, N_MEM), lambda bi, t, qi, kj, pt: (layer, srow(bi, t), 0, 0))
    whole = lambda a: pl.BlockSpec(a.shape, lambda bi, t, qi, kj, pt: (0,) * a.ndim)
    lf_table = pl.BlockSpec((None,) + lfc.shape[1:], lambda bi, t, qi, kj, pt: (layer, 0, 0, 0),
                            pipeline_mode=pl.Buffered(1))
    hbm = pl.BlockSpec(memory_space=pl.ANY)
    in_specs = [
        pl.BlockSpec((1, H_FOX, AUG, bq), lambda bi, t, qi, kj, pt: (bi, 0, 0, qi[t])),
        pl.BlockSpec((1, H_FOX, bq, AUG), lambda bi, t, qi, kj, pt: (bi, 0, kj[t], 0)),
        pl.BlockSpec((1, H_FOX, AUG, bq), lambda bi, t, qi, kj, pt: (bi, 0, 0, kj[t])),
        row(D_FOX), row(D_FOX), row(D_FOX), whole(lfT), row(D_REST), mem, mem, whole(tri_page),
        lf_table, hbm, hbm]
    page_buf = pltpu.VMEM((2, n_pages, D_FOX, PAGE_SIZE), F32)
    return pl.pallas_call(
        functools.partial(_attend_body, layer, n_pages, t_new, rows_per_b, nb),
        grid_spec=pltpu.PrefetchScalarGridSpec(
            num_scalar_prefetch=3,
            grid=(b, len(pairs)),
            in_specs=in_specs,
            out_specs=[pl.BlockSpec((1, bq, D_FOX), lambda bi, t, qi, kj, pt: (bi, qi[t], 0)),
                       row(D_FOX), row(D_XATTN)],
            scratch_shapes=[pltpu.VMEM((H_FOX, 8, bq), F32), pltpu.VMEM((H_FOX, AUG, bq), F32),
                            page_buf, page_buf, pltpu.SemaphoreType.DMA((2, 2))]),
        out_shape=[jax.ShapeDtypeStruct((b, s, D_FOX), F32),
                   jax.ShapeDtypeStruct((nb, t_new, D_FOX), F32),
                   jax.ShapeDtypeStruct((nb, t_new, D_XATTN), F32)],
        compiler_params=pltpu.CompilerParams(
            dimension_semantics=("arbitrary", "arbitrary"), vmem_limit_bytes=VMEM_LIMIT),
        name="attend",
    )(qi, kj, page_table, qTa, ka, vTa, q, knew, vnew, lfT, rest, mkc, mvc, tri_page, lfc, kc, vc)


def _finish_sample_body(t_new, wsp_ref, bsp_ref, x_ref, fo_ref, xo_ref, rest_ref, bn_ref, wout_ref,
                        gpost_ref, y_ref):
    n = x_ref.shape[0]
    u = rest_ref[:, D_XATTN:D_XATTN + D_GMLP]
    vg = rest_ref[:, D_XATTN + D_GMLP:D_XATTN + 2 * D_GMLP]
    gate = rest_ref[:, D_XATTN + 2 * D_GMLP:]

    pos = lax.broadcasted_iota(jnp.int32, (n, D_GMLP), 0) % t_new
    grp = lax.broadcasted_iota(jnp.int32, (n, D_GMLP), 1) // HEAD_DIM
    mixed = jnp.zeros((n, D_GMLP), F32)
    for g in range(G_GMLP):
        for i in range(t_new):
            mixed = jnp.where((grp == g) & (pos == i), bsp_ref[g, i], mixed)
    for k in range(t_new):
        coef = jnp.zeros((n, D_GMLP), F32)
        for g in range(G_GMLP):
            for i in range(k, t_new):
                coef = jnp.where((grp == g) & (pos == i), wsp_ref[g, i * t_new + i - k], coef)
        shifted = vg if k == 0 else pltpu.roll(vg, k, 0)
        mixed = mixed + coef * shifted
    go = u * mixed
    y_ref[...] = _merge_and_project(x_ref[...], fo_ref[...], go, xo_ref[...], gate, bn_ref, wout_ref,
                                    gpost_ref)


def _finish_sample(x, fo, xo, rest, w_small, b_small, w, t_new):
    n, d = x.shape
    full = lambda shape: pl.BlockSpec(shape, lambda i: (0,) * len(shape))
    smem = pl.BlockSpec(memory_space=pltpu.SMEM)
    return pl.pallas_call(
        functools.partial(_finish_sample_body, t_new),
        grid=(1,),
        in_specs=[smem, smem, full((n, d)), full((n, D_FOX)), full((n, D_XATTN)), full((n, D_REST)),
                  full((1, D_MIX)), full((D_MIX, d)), full((1, d))],
        out_specs=full((n, d)),
        out_shape=jax.ShapeDtypeStruct((n, d), F32),
        compiler_params=pltpu.CompilerParams(
            dimension_semantics=("arbitrary",), vmem_limit_bytes=VMEM_LIMIT),
        name="finish_sample",
    )(w_small, b_small, x, fo, xo, rest, w['bn'], w['wout'], w['gpost'])


def _constants(tm):
    f = jnp.arange(D_FOX)
    gsum = (f[:, None] // HEAD_DIM == f[None, :] // HEAD_DIM).astype(BF16)
    r = jnp.arange(32)
    fh, fo = f // HEAD_DIM, f % HEAD_DIM
    selq = jnp.where((fo[:, None] < 3) & (r[None, :] == fo[:, None] * 8 + fh[:, None]), 1.0,
                     jnp.where((fo[:, None] >= 3) & (fo[:, None] < 6) & (r[None, :] == 24), 1.0, 0.0)).astype(BF16)
    selk = jnp.where((fo[:, None] < 3) & (r[None, :] == 24), 1.0,
                     jnp.where((fo[:, None] >= 3) & (fo[:, None] < 6)
                               & (r[None, :] == (fo[:, None] - 3) * 8 + fh[:, None]), -1.0, 0.0)).astype(BF16)
    t = jnp.arange(tm)
    tri = (t[:, None] <= t[None, :]).astype(BF16)
    p = jnp.arange(PAGE_SIZE)
    tri_page = (p[:, None] <= p[None, :]).astype(BF16)
    return dict(gsum=gsum, selq=selq, selk=selk, tri=tri, tri_page=tri_page)


def _layer_weights(l, norm_pre, w_in, b_forget, q_norm, k_norm, gmlp_v_norm, w_spatial, b_spatial,
                   branch_norm, w_out, norm_post):
    d = w_in.shape[1]
    fl_cols = w_in[l][:, 3 * D_FOX + D_REST:]
    wflT = jnp.concatenate([fl_cols.T, jnp.zeros((8, d), F32)], axis=0).astype(BF16)
    return dict(
        gpre=norm_pre[l].reshape(1, d),
        wflT=wflT,
        bf=b_forget[l].reshape(H_FOX, 1),
        qg=(q_norm[l] * (SCALE * LOG2E)).reshape(HEAD_DIM, 1),
        qgrow=(jnp.tile(q_norm[l], H_FOX) * SCALE).reshape(1, D_FOX),
        kg=k_norm[l].reshape(HEAD_DIM, 1),
        kgrow=jnp.tile(k_norm[l], H_FOX).reshape(1, D_FOX),
        vgain=gmlp_v_norm[l].reshape(1, D_GMLP),
        wsp=w_spatial[l].reshape(G_GMLP * CHUNK, CHUNK),
        bsp=jnp.repeat(b_spatial[l].T, HEAD_DIM, axis=1),
        bn=branch_norm[l].reshape(1, D_MIX),
        wout=w_out[l].astype(BF16),
        gpost=norm_post[l].reshape(1, d),
    )


def kernel(x_prompt, x_sample, mem_prompt, cache_fox_k, cache_fox_v, cache_fox_lf, cache_mem_k, cache_mem_v,
           page_table, norm_pre, w_in, b_forget, q_norm, k_norm, gmlp_v_norm, w_spatial, b_spatial, mem_norm,
           w_mem_kv, branch_norm, w_out, norm_post):
    depth = w_in.shape[0]
    b, s, d = x_prompt.shape
    nb, t_new, _ = x_sample.shape
    n_pool = cache_fox_k.shape[1]
    tm_proj = min(TM_PROJ, s)
    tm_finish = min(TM_FINISH, s)
    bq = min(BQ, s)
    consts = _constants(tm_proj)

    kc = jnp.transpose(cache_fox_k, (0, 1, 3, 4, 2)).reshape(depth, n_pool, D_FOX, PAGE_SIZE)
    vc = jnp.transpose(cache_fox_v, (0, 1, 3, 4, 2)).reshape(depth, n_pool, D_FOX, PAGE_SIZE)
    lfc = jnp.transpose(cache_fox_lf, (0, 1, 3, 2))
    mkc = jnp.transpose(cache_mem_k, (0, 1, 3, 4, 2)).reshape(depth, nb, D_XATTN, N_MEM)
    mvc = jnp.transpose(cache_mem_v, (0, 1, 3, 4, 2)).reshape(depth, nb, D_XATTN, N_MEM)

    wmT = jnp.transpose(w_mem_kv, (0, 2, 1)).astype(BF16)
    mkT, mvT, mkbd, mvbd = _mem_kv(mem_prompt, mem_norm, wmT)

    wT = jnp.transpose(w_in, (0, 2, 1)).astype(BF16)

    xp = x_prompt
    xs = x_sample.reshape(nb * t_new, d)
    carried = (jnp.zeros((depth, b, H_FOX, HEAD_DIM, s), F32), jnp.zeros((depth, b, H_FOX, HEAD_DIM, s), F32),
               jnp.zeros((depth, b, H_FOX, s), F32))
    kss, vss, lfss, vgs = [], [], [], []
    for l in range(depth):
        w = _layer_weights(l, norm_pre, w_in, b_forget, q_norm, k_norm, gmlp_v_norm, w_spatial, b_spatial,
                           branch_norm, w_out, norm_post)
        qTa, ka, vTa, kT_all, vT_all, lfT_all, rest = _proj_prompt(xp, l, depth, wT, w, consts, tm_proj, carried)
        carried = (kT_all, vT_all, lfT_all)
        q_s, k_s, v_s, lfT_s, rest_s = _proj_sample(xs, l, wT, w, consts)
        r3 = lambda a: a.reshape(nb, t_new, a.shape[-1])
        o, fo, xo = _attend(l, qTa, ka, vTa, bq, page_table, r3(q_s), r3(k_s), r3(v_s), lfT_s, r3(rest_s),
                            kc, vc, lfc, mkc, mvc, consts['tri_page'])
        xp = _finish_prompt(xp, o, rest, mkbd, mvbd, l, w, tm_finish)
        w_small = w_spatial[l][:, :t_new, :t_new].reshape(G_GMLP, t_new * t_new)
        b_small = b_spatial[l][:, :t_new]
        xs = _finish_sample(xs, fo.reshape(nb * t_new, D_FOX), xo.reshape(nb * t_new, D_XATTN), rest_s,
                            w_small, b_small, w, t_new)
        kss.append(k_s); vss.append(v_s); lfss.append(lfT_s)
        vgs.append(rest_s[:, D_XATTN + D_GMLP:D_XATTN + 2 * D_GMLP])

    heads_last = lambda t: jnp.transpose(t, (0, 1, 4, 2, 3))
    mem_view = lambda m: jnp.transpose(m.reshape(depth, b, H_XATTN, HEAD_DIM, N_MEM), (0, 1, 4, 2, 3))
    kT_all, vT_all, lfT_all = carried
    return (
        xp,
        xs.reshape(nb, t_new, d),
        heads_last(kT_all),
        heads_last(vT_all),
        jnp.transpose(lfT_all, (0, 1, 3, 2)),
        mem_view(mkT),
        mem_view(mvT),
        jnp.stack(kss).reshape(depth, nb, t_new, H_FOX, HEAD_DIM),
        jnp.stack(vss).reshape(depth, nb, t_new, H_FOX, HEAD_DIM),
        jnp.transpose(jnp.stack(lfss), (0, 2, 1)).reshape(depth, nb, t_new, H_FOX),
        jnp.stack(vgs).reshape(depth, nb, t_new, D_GMLP),
    )
```

```python
import functools

import jax
import jax.numpy as jnp
from jax import lax
from jax.experimental import pallas as pl
from jax.experimental.pallas import tpu as pltpu

HEAD_DIM = 64
H_FOX = 8
D_FOX = H_FOX * HEAD_DIM
G_GMLP = 4
D_GMLP = G_GMLP * HEAD_DIM
H_XATTN = 4
D_XATTN = H_XATTN * HEAD_DIM
D_MIX = D_FOX + D_GMLP + D_XATTN
N_MEM = 256
CHUNK = 128
PAGE_SIZE = 128
EPS = 1e-6
NEG = -1e30
SCALE = HEAD_DIM ** -0.5
LOG2E = 1.4426950408889634
LANES = 128
AUG = 2 * HEAD_DIM
D_REST = D_XATTN + 2 * D_GMLP + D_MIX
VMEM_LIMIT = 56 * 1024 * 1024
TM_PROJ = 512
TM_FINISH = 512
FINISH_ROWS = 256
BQ = 512
QK_AHEAD = 2
DECODE_PAGES_PER_STAGE = 4

F32 = jnp.float32
BF16 = jnp.bfloat16

_NN = (((1,), (0,)), ((), ()))
_NT = (((1,), (1,)), ((), ()))
_TN = (((0,), (0,)), ((), ()))


def _dot(a, b, dims=_NN):
    return lax.dot_general(a, b, dims, preferred_element_type=F32)


def _split3(x):
    hi = x.astype(BF16).astype(F32)
    r = x - hi
    mid = r.astype(BF16).astype(F32)
    lo = (r - mid).astype(BF16).astype(F32)
    return hi, mid, lo


def _log_sigmoid(x):
    return jnp.minimum(x, 0.0) - jnp.log1p(jnp.exp(-jnp.abs(x)))


def _rms(x, gain):
    return x * lax.rsqrt(jnp.mean(x * x, axis=-1, keepdims=True) + EPS) * gain


def _cumsum_lanes(xT, tri):
    hi, mid, lo = _split3(xT)
    parts = jnp.concatenate([hi, mid, lo, jnp.zeros_like(hi)], axis=0).astype(BF16)
    cs = _dot(parts, tri)
    return cs[0:8] + cs[8:16] + cs[16:24]


def _proj_prompt_body(n_carried, x_ref, gpre_ref, wT_ref, wflT_ref, bf_ref, qg_ref, kg_ref, vgain_ref,
                      selq_ref, selk_ref, tri_ref, *refs):
    qTa_ref, ka_ref, vTa_ref, kT_ref, vT_ref, lfT_ref, rest_ref, carry_ref = refs[n_carried:]
    tm = x_ref.shape[1]
    if n_carried == 0:
        for ref in (kT_ref, vT_ref, lfT_ref):
            ref[1:] = jnp.zeros((ref.shape[0] - 1,) + ref.shape[1:], F32)
        kT_ref, vT_ref, lfT_ref = kT_ref.at[0], vT_ref.at[0], lfT_ref.at[0]

    @pl.when(pl.program_id(1) == 0)
    def _():
        carry_ref[...] = jnp.zeros_like(carry_ref)

    x = x_ref[0]
    h = _rms(x, gpre_ref[...]).astype(BF16)

    za = _dot(h, wT_ref[3 * D_FOX:3 * D_FOX + D_REST], _NT)
    rest_ref[0, :, 0:D_XATTN + D_GMLP] = za[:, 0:D_XATTN + D_GMLP]
    vg = za[:, D_XATTN + D_GMLP:D_XATTN + 2 * D_GMLP]
    rest_ref[0, :, D_XATTN + D_GMLP:D_XATTN + 2 * D_GMLP] = _rms(vg, vgain_ref[...])
    rest_ref[0, :, D_XATTN + 2 * D_GMLP:] = za[:, D_XATTN + 2 * D_GMLP:]

    zT = _dot(wT_ref[0:3 * D_FOX], h, _NT)
    flT = _dot(wflT_ref[...], h, _NT)
    lfT = _log_sigmoid(flT[0:8] + bf_ref[...])
    lfT_ref[0] = lfT

    cT = _cumsum_lanes(lfT, tri_ref[...]) + carry_ref[:, 0:1]
    carry_ref[...] = jnp.broadcast_to(cT[:, tm - 1:tm], carry_ref.shape)
    chi, cmid, clo = _split3(cT * LOG2E)
    cparts = jnp.concatenate([chi, cmid, clo, jnp.ones_like(chi)], axis=0).astype(BF16)

    qaugT = _dot(selq_ref[...], cparts)
    kaugT = _dot(selk_ref[...], cparts)
    row = lax.broadcasted_iota(jnp.int32, (HEAD_DIM, tm), 0)
    vaugT = jnp.where(row == 0, 1.0, 0.0).astype(F32)

    def head_norm(t, gain):
        return t * lax.rsqrt(jnp.mean(t * t, axis=0, keepdims=True) + EPS) * gain

    for hh in range(H_FOX):
        sl = slice(hh * HEAD_DIM, (hh + 1) * HEAD_DIM)
        qn = head_norm(zT[sl], qg_ref[...])
        qTa_ref[0, hh] = jnp.concatenate([qn, qaugT[sl]], axis=0).astype(BF16)
        kn = head_norm(zT[D_FOX + hh * HEAD_DIM:D_FOX + (hh + 1) * HEAD_DIM], kg_ref[...])
        kT_ref[0, hh] = kn
        ka_ref[0, hh] = jnp.concatenate([kn, kaugT[sl]], axis=0).T.astype(BF16)
        vh = zT[2 * D_FOX + hh * HEAD_DIM:2 * D_FOX + (hh + 1) * HEAD_DIM]
        vT_ref[0, hh] = vh
        vTa_ref[0, hh] = jnp.concatenate([vh, vaugT], axis=0).astype(BF16)


def _proj_prompt(x, layer, depth, wT, w, consts, tm, carried):
    b, s, d = x.shape
    n_t = s // tm
    full = lambda shape: pl.BlockSpec(shape, lambda bi, ti: (0,) * len(shape), pipeline_mode=pl.Buffered(1))
    in_specs = [
        pl.BlockSpec((1, tm, d), lambda bi, ti: (bi, ti, 0)),
        full((1, d)),
        pl.BlockSpec((None, 3 * D_FOX + D_REST, d), lambda bi, ti: (layer, 0, 0), pipeline_mode=pl.Buffered(1)),
        full(w['wflT'].shape),
        full((8, 1)), full((HEAD_DIM, 1)), full((HEAD_DIM, 1)), full((1, D_GMLP)),
        full(consts['selq'].shape), full(consts['selk'].shape), full(consts['tri'].shape),
    ]
    operands = [x, w['gpre'], wT, w['wflT'], w['bf'], w['qg'], w['kg'], w['vgain'],
                consts['selq'], consts['selk'], consts['tri']]
    aliases = {}
    for n, buf in enumerate(carried or ()):
        aliases[len(operands)] = 3 + n
        in_specs.append(pl.BlockSpec(memory_space=pl.ANY))
        operands.append(buf)
    lead = None if carried else depth
    lsel = layer if carried else 0
    out_shape = [
        jax.ShapeDtypeStruct((b, H_FOX, AUG, s), BF16),
        jax.ShapeDtypeStruct((b, H_FOX, s, AUG), BF16),
        jax.ShapeDtypeStruct((b, H_FOX, AUG, s), BF16),
        jax.ShapeDtypeStruct((depth, b, H_FOX, HEAD_DIM, s), F32),
        jax.ShapeDtypeStruct((depth, b, H_FOX, HEAD_DIM, s), F32),
        jax.ShapeDtypeStruct((depth, b, H_FOX, s), F32),
        jax.ShapeDtypeStruct((b, s, D_REST), F32),
    ]
    out_specs = [
        pl.BlockSpec((1, H_FOX, AUG, tm), lambda bi, ti: (bi, 0, 0, ti)),
        pl.BlockSpec((1, H_FOX, tm, AUG), lambda bi, ti: (bi, 0, ti, 0)),
        pl.BlockSpec((1, H_FOX, AUG, tm), lambda bi, ti: (bi, 0, 0, ti)),
        pl.BlockSpec((lead, 1, H_FOX, HEAD_DIM, tm), lambda bi, ti: (lsel, bi, 0, 0, ti)),
        pl.BlockSpec((lead, 1, H_FOX, HEAD_DIM, tm), lambda bi, ti: (lsel, bi, 0, 0, ti)),
        pl.BlockSpec((lead, 1, H_FOX, tm), lambda bi, ti: (lsel, bi, 0, ti)),
        pl.BlockSpec((1, tm, D_REST), lambda bi, ti: (bi, ti, 0)),
    ]
    return pl.pallas_call(
        functools.partial(_proj_prompt_body, len(aliases)),
        grid=(b, n_t),
        in_specs=in_specs, out_specs=out_specs, out_shape=out_shape,
        scratch_shapes=[pltpu.VMEM((8, LANES), F32)],
        input_output_aliases=aliases,
        compiler_params=pltpu.CompilerParams(
            dimension_semantics=("arbitrary", "arbitrary"), vmem_limit_bytes=VMEM_LIMIT),
        name="proj_prompt",
    )(*operands)


def _fox_tile(i, j, qTa_ref, ka_ref, vTa_ref, o_ref, m_ref, acc_ref, has_side, side_work):
    bq = qTa_ref.shape[3]
    bk = ka_ref.shape[2]

    @pl.when(j == 0)
    def _():
        m_ref[...] = jnp.full_like(m_ref, NEG)
        acc_ref[...] = jnp.zeros_like(acc_ref)

    def all_heads(diagonal, with_side):
        if diagonal:
            keep = (lax.broadcasted_iota(jnp.int32, (bk, bq), 0)
                    <= lax.broadcasted_iota(jnp.int32, (bk, bq), 1))
        side = side_work() if with_side else iter(())
        sTs = {h: _dot(ka_ref[0, h], qTa_ref[0, h]) for h in range(QK_AHEAD)}
        for hh in range(H_FOX):
            sT = sTs.pop(hh)
            if diagonal:
                sT = jnp.where(keep, sT, NEG)
            m_prev = m_ref[hh][0:1]
            m_new = jnp.maximum(m_prev, jnp.max(sT, axis=0, keepdims=True))
            alpha = jnp.exp2(m_prev - m_new)
            pT = jnp.exp2(sT - m_new).astype(BF16)
            if hh + QK_AHEAD < H_FOX:
                sTs[hh + QK_AHEAD] = _dot(ka_ref[0, hh + QK_AHEAD], qTa_ref[0, hh + QK_AHEAD])
            next(side, None)
            pv = _dot(vTa_ref[0, hh], pT)
            acc_ref[hh] = alpha * acc_ref[hh] + pv
            m_ref[hh] = jnp.broadcast_to(m_new, (8, bq))
        for _ in side:
            pass

    def finalize():
        lane = lax.broadcasted_iota(jnp.int32, (bq, LANES), 1)
        for pair in range(H_FOX // 2):
            a0 = acc_ref[2 * pair]
            a1 = acc_ref[2 * pair + 1]
            o0 = (a0 * (1.0 / a0[HEAD_DIM:HEAD_DIM + 1])).T
            o1 = (a1 * (1.0 / a1[HEAD_DIM:HEAD_DIM + 1])).T
            o_ref[0, :, pair * LANES:(pair + 1) * LANES] = jnp.where(
                lane < HEAD_DIM, o0, pltpu.roll(o1, HEAD_DIM, 1))

    for with_side in (False, True):
        side_now = has_side if with_side else jnp.logical_not(has_side)

        @pl.when((j < i) & side_now)
        def _():
            all_heads(False, with_side)

        @pl.when((j == i) & side_now)
        def _():
            all_heads(True, with_side)
            finalize()


def _mem_kv_body(mem_ref, gain_ref, wT_ref, mkT_ref, mvT_ref, mkbd_ref, mvbd_ref):
    hm = _rms(mem_ref[0], gain_ref[0]).astype(BF16)
    kvT = _dot(wT_ref[0], hm, _NT)
    mkT = kvT[:D_XATTN]
    mvT = kvT[D_XATTN:]
    mkT_ref[0, 0] = mkT
    mvT_ref[0, 0] = mvT
    mkbd_ref[...] = jnp.zeros_like(mkbd_ref)
    mvbd_ref[...] = jnp.zeros_like(mvbd_ref)
    for hh in range(H_XATTN):
        r = slice(hh * HEAD_DIM, (hh + 1) * HEAD_DIM)
        c = slice(hh * N_MEM, (hh + 1) * N_MEM)
        mkbd_ref[0, 0, r, c] = mkT[r].astype(BF16)
        mvbd_ref[0, 0, r, c] = mvT[r].astype(BF16)


def _mem_kv(mem, mem_norm, wmT):
    b, n_mem, d = mem.shape
    depth = wmT.shape[0]
    blk = lambda shape: pl.BlockSpec((1, 1) + shape, lambda l, bi: (l, bi, 0, 0))
    return pl.pallas_call(
        _mem_kv_body,
        grid=(depth, b),
        in_specs=[
            pl.BlockSpec((1, n_mem, d), lambda l, bi: (bi, 0, 0)),
            pl.BlockSpec((1, 1, d), lambda l, bi: (l, 0, 0)),
            pl.BlockSpec((1, 2 * D_XATTN, d), lambda l, bi: (l, 0, 0)),
        ],
        out_specs=[blk((D_XATTN, n_mem)), blk((D_XATTN, n_mem)),
                   blk((D_XATTN, H_XATTN * n_mem)), blk((D_XATTN, H_XATTN * n_mem))],
        out_shape=[
            jax.ShapeDtypeStruct((depth, b, D_XATTN, n_mem), F32),
            jax.ShapeDtypeStruct((depth, b, D_XATTN, n_mem), F32),
            jax.ShapeDtypeStruct((depth, b, D_XATTN, H_XATTN * n_mem), BF16),
            jax.ShapeDtypeStruct((depth, b, D_XATTN, H_XATTN * n_mem), BF16),
        ],
        compiler_params=pltpu.CompilerParams(
            dimension_semantics=("arbitrary", "arbitrary"), vmem_limit_bytes=VMEM_LIMIT),
        name="mem_kv",
    )(mem, mem_norm.reshape(depth, 1, d), wmT)


def _merge_and_project(x, fo, go, xo, gate, bn_ref, wout_ref, gpost_ref):
    merged = jnp.concatenate([
        _rms(fo, bn_ref[:, 0:D_FOX]),
        _rms(go, bn_ref[:, D_FOX:D_FOX + D_GMLP]),
        _rms(xo, bn_ref[:, D_FOX + D_GMLP:]),
    ], axis=-1) * (gate * jax.nn.sigmoid(gate))
    y = _dot(merged.astype(BF16), wout_ref[...])
    return x + _rms(y, gpost_ref[...])


def _finish_rows(rows, w_tril, grp, x_ref, o_ref, rest_ref, mkbd_ref, mvbd_ref, bsp_ref, bn_ref, wout_ref,
                 gpost_ref, y_ref):
    n = rows.stop - rows.start
    rest = rest_ref[0, rows]
    qx = rest[:, 0:D_XATTN]
    u = rest[:, D_XATTN:D_XATTN + D_GMLP]
    vg = rest[:, D_XATTN + D_GMLP:D_XATTN + 2 * D_GMLP]
    gate = rest[:, D_XATTN + 2 * D_GMLP:]
    malls = [_dot(w_tril, vg[c * CHUNK:(c + 1) * CHUNK].astype(BF16)) for c in range(n // CHUNK)]
    s = _dot(qx.astype(BF16), mkbd_ref[0, 0]) * SCALE
    yield
    gos = []
    for c, mall in enumerate(malls):
        mixed = mall[0:CHUNK]
        for g in range(1, G_GMLP):
            mixed = jnp.where(grp == g, mall[g * CHUNK:(g + 1) * CHUNK], mixed)
        gos.append(u[c * CHUNK:(c + 1) * CHUNK] * (mixed + bsp_ref[...]))
    go = jnp.concatenate(gos, axis=0)
    ps = []
    for hh in range(H_XATTN):
        sh = s[:, hh * N_MEM:(hh + 1) * N_MEM]
        e = jnp.exp(sh - jnp.max(sh, axis=-1, keepdims=True))
        ps.append(e / jnp.sum(e, axis=-1, keepdims=True))
    p = jnp.concatenate(ps, axis=-1).astype(BF16)
    yield
    xo = _dot(p, mvbd_ref[0, 0], _NT)
    y_ref[0, rows] = _merge_and_project(x_ref[0, rows], o_ref[0, rows], go, xo, gate, bn_ref, wout_ref, gpost_ref)


def _finish_prompt_body(x_ref, o_ref, rest_ref, mkbd_ref, mvbd_ref, wsp_ref, bsp_ref, bn_ref, wout_ref,
                        gpost_ref, y_ref):
    tm = x_ref.shape[1]
    wi = lax.broadcasted_iota(jnp.int32, wsp_ref.shape, 0) % CHUNK
    wj = lax.broadcasted_iota(jnp.int32, wsp_ref.shape, 1)
    w_tril = jnp.where(wj <= wi, wsp_ref[...], 0.0).astype(BF16)
    grp = lax.broadcasted_iota(jnp.int32, (CHUNK, D_GMLP), 1) // HEAD_DIM
    waiting = [_finish_rows(slice(r0, r0 + FINISH_ROWS), w_tril, grp, x_ref, o_ref, rest_ref, mkbd_ref, mvbd_ref,
                            bsp_ref, bn_ref, wout_ref, gpost_ref, y_ref) for r0 in range(0, tm, FINISH_ROWS)]
    live = []
    while waiting or live:
        if waiting:
            live.append(waiting.pop(0))
        for g in list(live):
            if next(g, StopIteration) is StopIteration:
                live.remove(g)


def _finish_prompt(x, o, rest, mkbd, mvbd, layer, w, tm):
    b, s, d = x.shape
    full = lambda shape: pl.BlockSpec(shape, lambda bi, ti: (0,) * len(shape))
    tile = lambda width: pl.BlockSpec((1, tm, width), lambda bi, ti: (bi, ti, 0))
    kvspec = pl.BlockSpec((1, 1, D_XATTN, H_XATTN * N_MEM), lambda bi, ti: (layer, bi, 0, 0))
    return pl.pallas_call(
        _finish_prompt_body,
        grid=(b, s // tm),
        in_specs=[tile(d), tile(D_FOX), tile(D_REST), kvspec, kvspec,
                  full((G_GMLP * CHUNK, CHUNK)), full((CHUNK, D_GMLP)), full((1, D_MIX)),
                  full((D_MIX, d)), full((1, d))],
        out_specs=tile(d),
        out_shape=jax.ShapeDtypeStruct((b, s, d), F32),
        compiler_params=pltpu.CompilerParams(
            dimension_semantics=("arbitrary", "arbitrary"), vmem_limit_bytes=VMEM_LIMIT),
        name="finish_prompt",
    )(x, o, rest, mkbd, mvbd, w['wsp'], w['bsp'], w['bn'], w['wout'], w['gpost'])


def _proj_sample_body(x_ref, gpre_ref, wT_ref, wflT_ref, bf_ref, qg_ref, kgrow_ref, vgain_ref, gsum_ref,
                      q_ref, k_ref, v_ref, lfT_ref, rest_ref):
    h = _rms(x_ref[...], gpre_ref[...]).astype(BF16)
    z = _dot(h, wT_ref[...], _NT)

    def head_norm(t, gain):
        t2 = t * t
        t2h = t2.astype(BF16)
        t2l = (t2 - t2h.astype(F32)).astype(BF16)
        msq = (_dot(t2h, gsum_ref[...]) + _dot(t2l, gsum_ref[...])) * (1.0 / HEAD_DIM)
        return t * lax.rsqrt(msq + EPS) * gain

    q_ref[...] = head_norm(z[:, 0:D_FOX], qg_ref[...]).astype(BF16)
    k_ref[...] = head_norm(z[:, D_FOX:2 * D_FOX], kgrow_ref[...])
    v_ref[...] = z[:, 2 * D_FOX:3 * D_FOX]
    flT = _dot(wflT_ref[...], h, _NT)
    lfT_ref[...] = _log_sigmoid(flT[0:8] + bf_ref[...])
    base = 3 * D_FOX
    rest_ref[:, 0:D_XATTN + D_GMLP] = z[:, base:base + D_XATTN + D_GMLP]
    vg = z[:, base + D_XATTN + D_GMLP:base + D_XATTN + 2 * D_GMLP]
    rest_ref[:, D_XATTN + D_GMLP:D_XATTN + 2 * D_GMLP] = _rms(vg, vgain_ref[...])
    rest_ref[:, D_XATTN + 2 * D_GMLP:] = z[:, base + D_XATTN + 2 * D_GMLP:]


def _proj_sample(x, layer, wT, w, consts):
    n, d = x.shape
    full = lambda shape: pl.BlockSpec(shape, lambda i: (0,) * len(shape))
    return pl.pallas_call(
        _proj_sample_body,
        grid=(1,),
        in_specs=[full((n, d)), full((1, d)),
                  pl.BlockSpec((None, 3 * D_FOX + D_REST, d), lambda i: (layer, 0, 0)),
                  full(w['wflT'].shape), full((8, 1)),
                  full((1, D_FOX)), full((1, D_FOX)), full((1, D_GMLP)), full(consts['gsum'].shape)],
        out_specs=[full((n, D_FOX)), full((n, D_FOX)), full((n, D_FOX)), full((8, n)), full((n, D_REST))],
        out_shape=[
            jax.ShapeDtypeStruct((n, D_FOX), BF16),
            jax.ShapeDtypeStruct((n, D_FOX), F32),
            jax.ShapeDtypeStruct((n, D_FOX), F32),
            jax.ShapeDtypeStruct((8, n), F32),
            jax.ShapeDtypeStruct((n, D_REST), F32),
        ],
        compiler_params=pltpu.CompilerParams(
            dimension_semantics=("arbitrary",), vmem_limit_bytes=VMEM_LIMIT),
        name="proj_sample",
    )(x, w['gpre'], wT, w['wflT'], w['bf'], w['qgrow'], w['kgrow'], w['vgain'], consts['gsum'])


def _expand_rows(x4, width):
    t = x4.shape[0]
    rep = jnp.concatenate([jnp.broadcast_to(x4[i:i + 1], (8, width)) for i in range(t)], axis=0)
    rowh = lax.broadcasted_iota(jnp.int32, (8 * t, width), 0) % 8
    colh = lax.broadcasted_iota(jnp.int32, (8 * t, width), 1) // HEAD_DIM
    return jnp.where(rowh == colh, rep, jnp.zeros_like(rep))


def _collapse_rows(x, t):
    width = x.shape[1]
    rowh = lax.broadcasted_iota(jnp.int32, x.shape, 0) % 8
    colh = lax.broadcasted_iota(jnp.int32, x.shape, 1) // HEAD_DIM
    xm = jnp.where(rowh == colh, x, 0.0)
    return jnp.concatenate([jnp.sum(xm[i * 8:(i + 1) * 8], axis=0, keepdims=True) for i in range(t)], axis=0)


def _decode_row(b, t_new, q_ref, knew_ref, vnew_ref, lfT_ref, rest_ref, mkT_ref, mvT_ref, tri_ref,
                k_refs, v_refs, lf_refs, fo_ref, xo_ref):
    n_pages = len(k_refs)
    nrow = 8 * t_new
    n_tok = lfT_ref.shape[1]

    qbd = _expand_rows(q_ref[...].astype(F32), D_FOX).astype(BF16)

    lf_all = jnp.concatenate([lf_refs[p][...] for p in range(n_pages)], axis=0)
    hi, mid, lo = _split3(lf_all)
    cs = _dot(jnp.concatenate([hi, mid, lo], axis=0).astype(BF16), tri_ref[...])
    np8 = 8 * n_pages
    c_in = cs[0:np8] + cs[np8:2 * np8] + cs[2 * np8:3 * np8]
    carry = jnp.zeros((8, 1), F32)
    cks = []
    for p in range(n_pages):
        cks.append(c_in[8 * p:8 * p + 8] + carry)
        carry = carry + c_in[8 * p:8 * p + 8, PAGE_SIZE - 1:PAGE_SIZE]
    jj = lax.broadcasted_iota(jnp.int32, (n_tok, LANES), 0)
    tt = lax.broadcasted_iota(jnp.int32, (n_tok, LANES), 1)
    sel = ((jj >= b * t_new) & (jj <= b * t_new + tt) & (tt < t_new)).astype(BF16)
    cn = _cumsum_lanes(lfT_ref[...], sel) + carry
    cq = jnp.concatenate([cn[:, i:i + 1] for i in range(t_new)], axis=0)

    ss = []
    for p in range(n_pages):
        ss.append(_dot(qbd, k_refs[p][...].astype(BF16)) + (cq - jnp.concatenate([cks[p]] * t_new, axis=0)))
        if p % DECODE_PAGES_PER_STAGE == DECODE_PAGES_PER_STAGE - 1 and p + 1 < n_pages:
            yield
    pad = jnp.zeros((16 - t_new, D_FOX), F32)
    knew = jnp.concatenate([knew_ref[...], pad], axis=0).astype(BF16)
    vnew = jnp.concatenate([vnew_ref[...], pad], axis=0).astype(BF16)
    sn = _dot(qbd, knew, _NT) + (cq - jnp.concatenate([cn[:, 0:16]] * t_new, axis=0))
    qt = lax.broadcasted_iota(jnp.int32, (nrow, 16), 0) // 8
    kt = lax.broadcasted_iota(jnp.int32, (nrow, 16), 1)
    sn = jnp.where(kt <= qt, sn, NEG)

    smax = ss[0]
    for s in ss[1:]:
        smax = jnp.maximum(smax, s)
    m = jnp.maximum(jnp.max(smax, axis=-1, keepdims=True), jnp.max(sn, axis=-1, keepdims=True))
    en = jnp.exp(sn - m)
    acc = _dot(en.astype(BF16), vnew)
    esum = jnp.zeros((nrow, PAGE_SIZE), F32)
    for p in range(n_pages):
        if p % DECODE_PAGES_PER_STAGE == 0:
            yield
        e = jnp.exp(ss[p] - m)
        esum = esum + e
        acc = acc + _dot(e.astype(BF16), v_refs[p][...].astype(BF16), _NT)
    l = jnp.sum(esum, axis=-1, keepdims=True) + jnp.sum(en, axis=-1, keepdims=True)
    fo_ref[...] = _collapse_rows(acc * (1.0 / l), t_new)

    qx = rest_ref[:, 0:D_XATTN]
    qxbd = _expand_rows(qx, D_XATTN).astype(BF16)
    sx = _dot(qxbd, mkT_ref[...].astype(BF16)) * SCALE
    ex = jnp.exp(sx - jnp.max(sx, axis=-1, keepdims=True))
    px = (ex / jnp.sum(ex, axis=-1, keepdims=True)).astype(BF16)
    xo_ref[...] = _collapse_rows(_dot(px, mvT_ref[...].astype(BF16), _NT), t_new)


def _attend_body(layer, n_pages, t_new, rows_per_b, n_rows, qi_ref, kj_ref, pt_ref, qTa_ref, ka_ref, vTa_ref,
                 q_ref, knew_ref, vnew_ref, lfT_ref, rest_ref, mkT_ref, mvT_ref, tri_ref, lfc_ref, kc_ref, vc_ref,
                 o_ref, fo_ref, xo_ref, m_ref, acc_ref, kbuf, vbuf, sem):
    t = pl.program_id(1)
    row = pl.program_id(0) * rows_per_b + t
    has_row = t < rows_per_b

    def page_copies(r, slot):
        copies = []
        for p in range(n_pages):
            page = pt_ref[r, p]
            copies.append(pltpu.make_async_copy(kc_ref.at[layer, page], kbuf.at[slot, p], sem.at[slot, 0]))
            copies.append(pltpu.make_async_copy(vc_ref.at[layer, page], vbuf.at[slot, p], sem.at[slot, 1]))
        return copies

    @pl.when(row == 0)
    def _():
        for c in page_copies(0, 0):
            c.start()

    @pl.when(has_row & (row + 1 < n_rows))
    def _():
        for c in page_copies(row + 1, (row + 1) % 2):
            c.start()

    @pl.when(has_row)
    def _():
        for c in page_copies(row, row % 2):
            c.wait()

    def sample_row():
        slot = row % 2
        k_pages = [kbuf.at[slot, p] for p in range(n_pages)]
        v_pages = [vbuf.at[slot, p] for p in range(n_pages)]
        lf_pages = [lfc_ref.at[pt_ref[row, p]] for p in range(n_pages)]
        return _decode_row(row, t_new, q_ref, knew_ref, vnew_ref, lfT_ref, rest_ref, mkT_ref, mvT_ref, tri_ref,
                           k_pages, v_pages, lf_pages, fo_ref, xo_ref)

    _fox_tile(qi_ref[t], kj_ref[t], qTa_ref, ka_ref, vTa_ref, o_ref, m_ref, acc_ref, has_row, sample_row)


def _attend(layer, qTa, ka, vTa, bq, page_table, q, knew, vnew, lfT, rest, kc, vc, lfc, mkc, mvc, tri_page):
    b, _, _, s = qTa.shape
    nb, t_new, _ = q.shape
    n_pages = page_table.shape[1]
    nq = s // bq
    pairs = [(i, j) for i in range(nq) for j in range(i + 1)]
    qi = jnp.array([p[0] for p in pairs], jnp.int32)
    kj = jnp.array([p[1] for p in pairs], jnp.int32)
    assert nb % b == 0 and nb // b <= len(pairs), (nb, b, len(pairs))
    rows_per_b = nb // b
    srow = lambda bi, t: bi * rows_per_b + jnp.minimum(t, rows_per_b - 1)

    row = lambda width: pl.BlockSpec((None, t_new, width), lambda bi, t, qi, kj, pt: (srow(bi, t), 0, 0))
    mem = pl.BlockSpec((None, None, D_XATTN, N_MEM), lambda bi, t, qi, kj, pt: (layer, srow(bi, t), 0, 0))
    whole = lambda a: pl.BlockSpec(a.shape, lambda bi, t, qi, kj, pt: (0,) * a.ndim)
    lf_table = pl.BlockSpec((None,) + lfc.shape[1:], lambda bi, t, qi, kj, pt: (layer, 0, 0, 0),
                            pipeline_mode=pl.Buffered(1))
    hbm = pl.BlockSpec(memory_space=pl.ANY)
    in_specs = [
        pl.BlockSpec((1, H_FOX, AUG, bq), lambda bi, t, qi, kj, pt: (bi, 0, 0, qi[t])),
        pl.BlockSpec((1, H_FOX, bq, AUG), lambda bi, t, qi, kj, pt: (bi, 0, kj[t], 0)),
        pl.BlockSpec((1, H_FOX, AUG, bq), lambda bi, t, qi, kj, pt: (bi, 0, 0, kj[t])),
        row(D_FOX), row(D_FOX), row(D_FOX), whole(lfT), row(D_REST), mem, mem, whole(tri_page),
        lf_table, hbm, hbm]
    page_buf = pltpu.VMEM((2, n_pages, D_FOX, PAGE_SIZE), F32)
    return pl.pallas_call(
        functools.partial(_attend_body, layer, n_pages, t_new, rows_per_b, nb),
        grid_spec=pltpu.PrefetchScalarGridSpec(
            num_scalar_prefetch=3,
            grid=(b, len(pairs)),
            in_specs=in_specs,
            out_specs=[pl.BlockSpec((1, bq, D_FOX), lambda bi, t, qi, kj, pt: (bi, qi[t], 0)),
                       row(D_FOX), row(D_XATTN)],
            scratch_shapes=[pltpu.VMEM((H_FOX, 8, bq), F32), pltpu.VMEM((H_FOX, AUG, bq), F32),
                            page_buf, page_buf, pltpu.SemaphoreType.DMA((2, 2))]),
        out_shape=[jax.ShapeDtypeStruct((b, s, D_FOX), F32),
                   jax.ShapeDtypeStruct((nb, t_new, D_FOX), F32),
                   jax.ShapeDtypeStruct((nb, t_new, D_XATTN), F32)],
        compiler_params=pltpu.CompilerParams(
            dimension_semantics=("arbitrary", "arbitrary"), vmem_limit_bytes=VMEM_LIMIT),
        name="attend",
    )(qi, kj, page_table, qTa, ka, vTa, q, knew, vnew, lfT, rest, mkc, mvc, tri_page, lfc, kc, vc)


def _finish_sample_body(t_new, wsp_ref, bsp_ref, x_ref, fo_ref, xo_ref, rest_ref, bn_ref, wout_ref,
                        gpost_ref, y_ref):
    n = x_ref.shape[0]
    u = rest_ref[:, D_XATTN:D_XATTN + D_GMLP]
    vg = rest_ref[:, D_XATTN + D_GMLP:D_XATTN + 2 * D_GMLP]
    gate = rest_ref[:, D_XATTN + 2 * D_GMLP:]

    pos = lax.broadcasted_iota(jnp.int32, (n, D_GMLP), 0) % t_new
    grp = lax.broadcasted_iota(jnp.int32, (n, D_GMLP), 1) // HEAD_DIM
    mixed = jnp.zeros((n, D_GMLP), F32)
    for g in range(G_GMLP):
        for i in range(t_new):
            mixed = jnp.where((grp == g) & (pos == i), bsp_ref[g, i], mixed)
    for k in range(t_new):
        coef = jnp.zeros((n, D_GMLP), F32)
        for g in range(G_GMLP):
            for i in range(k, t_new):
                coef = jnp.where((grp == g) & (pos == i), wsp_ref[g, i * t_new + i - k], coef)
        shifted = vg if k == 0 else pltpu.roll(vg, k, 0)
        mixed = mixed + coef * shifted
    go = u * mixed
    y_ref[...] = _merge_and_project(x_ref[...], fo_ref[...], go, xo_ref[...], gate, bn_ref, wout_ref,
                                    gpost_ref)


def _finish_sample(x, fo, xo, rest, w_small, b_small, w, t_new):
    n, d = x.shape
    full = lambda shape: pl.BlockSpec(shape, lambda i: (0,) * len(shape))
    smem = pl.BlockSpec(memory_space=pltpu.SMEM)
    return pl.pallas_call(
        functools.partial(_finish_sample_body, t_new),
        grid=(1,),
        in_specs=[smem, smem, full((n, d)), full((n, D_FOX)), full((n, D_XATTN)), full((n, D_REST)),
                  full((1, D_MIX)), full((D_MIX, d)), full((1, d))],
        out_specs=full((n, d)),
        out_shape=jax.ShapeDtypeStruct((n, d), F32),
        compiler_params=pltpu.CompilerParams(
            dimension_semantics=("arbitrary",), vmem_limit_bytes=VMEM_LIMIT),
        name="finish_sample",
    )(w_small, b_small, x, fo, xo, rest, w['bn'], w['wout'], w['gpost'])


def _constants(tm):
    f = jnp.arange(D_FOX)
    gsum = (f[:, None] // HEAD_DIM == f[None, :] // HEAD_DIM).astype(BF16)
    r = jnp.arange(32)
    fh, fo = f // HEAD_DIM, f % HEAD_DIM
    selq = jnp.where((fo[:, None] < 3) & (r[None, :] == fo[:, None] * 8 + fh[:, None]), 1.0,
                     jnp.where((fo[:, None] >= 3) & (fo[:, None] < 6) & (r[None, :] == 24), 1.0, 0.0)).astype(BF16)
    selk = jnp.where((fo[:, None] < 3) & (r[None, :] == 24), 1.0,
                     jnp.where((fo[:, None] >= 3) & (fo[:, None] < 6)
                               & (r[None, :] == (fo[:, None] - 3) * 8 + fh[:, None]), -1.0, 0.0)).astype(BF16)
    t = jnp.arange(tm)
    tri = (t[:, None] <= t[None, :]).astype(BF16)
    p = jnp.arange(PAGE_SIZE)
    tri_page = (p[:, None] <= p[None, :]).astype(BF16)
    return dict(gsum=gsum, selq=selq, selk=selk, tri=tri, tri_page=tri_page)


def _layer_weights(l, norm_pre, w_in, b_forget, q_norm, k_norm, gmlp_v_norm, w_spatial, b_spatial,
                   branch_norm, w_out, norm_post):
    d = w_in.shape[1]
    fl_cols = w_in[l][:, 3 * D_FOX + D_REST:]
    wflT = jnp.concatenate([fl_cols.T, jnp.zeros((8, d), F32)], axis=0).astype(BF16)
    return dict(
        gpre=norm_pre[l].reshape(1, d),
        wflT=wflT,
        bf=b_forget[l].reshape(H_FOX, 1),
        qg=(q_norm[l] * (SCALE * LOG2E)).reshape(HEAD_DIM, 1),
        qgrow=(jnp.tile(q_norm[l], H_FOX) * SCALE).reshape(1, D_FOX),
        kg=k_norm[l].reshape(HEAD_DIM, 1),
        kgrow=jnp.tile(k_norm[l], H_FOX).reshape(1, D_FOX),
        vgain=gmlp_v_norm[l].reshape(1, D_GMLP),
        wsp=w_spatial[l].reshape(G_GMLP * CHUNK, CHUNK),
        bsp=jnp.repeat(b_spatial[l].T, HEAD_DIM, axis=1),
        bn=branch_norm[l].reshape(1, D_MIX),
        wout=w_out[l].astype(BF16),
        gpost=norm_post[l].reshape(1, d),
    )


def kernel(x_prompt, x_sample, mem_prompt, cache_fox_k, cache_fox_v, cache_fox_lf, cache_mem_k, cache_mem_v,
           page_table, norm_pre, w_in, b_forget, q_norm, k_norm, gmlp_v_norm, w_spatial, b_spatial, mem_norm,
           w_mem_kv, branch_norm, w_out, norm_post):
    depth = w_in.shape[0]
    b, s, d = x_prompt.shape
    nb, t_new, _ = x_sample.shape
    n_pool = cache_fox_k.shape[1]
    tm_proj = min(TM_PROJ, s)
    tm_finish = min(TM_FINISH, s)
    bq = min(BQ, s)
    consts = _constants(tm_proj)

    kc = jnp.transpose(cache_fox_k, (0, 1, 3, 4, 2)).reshape(depth, n_pool, D_FOX, PAGE_SIZE)
    vc = jnp.transpose(cache_fox_v, (0, 1, 3, 4, 2)).reshape(depth, n_pool, D_FOX, PAGE_SIZE)
    lfc = jnp.transpose(cache_fox_lf, (0, 1, 3, 2))
    mkc = jnp.transpose(cache_mem_k, (0, 1, 3, 4, 2)).reshape(depth, nb, D_XATTN, N_MEM)
    mvc = jnp.transpose(cache_mem_v, (0, 1, 3, 4, 2)).reshape(depth, nb, D_XATTN, N_MEM)

    wmT = jnp.transpose(w_mem_kv, (0, 2, 1)).astype(BF16)
    mkT, mvT, mkbd, mvbd = _mem_kv(mem_prompt, mem_norm, wmT)

    wT = jnp.transpose(w_in, (0, 2, 1)).astype(BF16)

    xp = x_prompt
    xs = x_sample.reshape(nb * t_new, d)
    carried = None
    kss, vss, lfss, vgs = [], [], [], []
    for l in range(depth):
        w = _layer_weights(l, norm_pre, w_in, b_forget, q_norm, k_norm, gmlp_v_norm, w_spatial, b_spatial,
                           branch_norm, w_out, norm_post)
        qTa, ka, vTa, kT_all, vT_all, lfT_all, rest = _proj_prompt(xp, l, depth, wT, w, consts, tm_proj, carried)
        carried = (kT_all, vT_all, lfT_all)
        q_s, k_s, v_s, lfT_s, rest_s = _proj_sample(xs, l, wT, w, consts)
        r3 = lambda a: a.reshape(nb, t_new, a.shape[-1])
        o, fo, xo = _attend(l, qTa, ka, vTa, bq, page_table, r3(q_s), r3(k_s), r3(v_s), lfT_s, r3(rest_s),
                            kc, vc, lfc, mkc, mvc, consts['tri_page'])
        xp = _finish_prompt(xp, o, rest, mkbd, mvbd, l, w, tm_finish)
        w_small = w_spatial[l][:, :t_new, :t_new].reshape(G_GMLP, t_new * t_new)
        b_small = b_spatial[l][:, :t_new]
        xs = _finish_sample(xs, fo.reshape(nb * t_new, D_FOX), xo.reshape(nb * t_new, D_XATTN), rest_s,
                            w_small, b_small, w, t_new)
        kss.append(k_s); vss.append(v_s); lfss.append(lfT_s)
        vgs.append(rest_s[:, D_XATTN + D_GMLP:D_XATTN + 2 * D_GMLP])

    heads_last = lambda t: jnp.transpose(t, (0, 1, 4, 2, 3))
    mem_view = lambda m: jnp.transpose(m.reshape(depth, b, H_XATTN, HEAD_DIM, N_MEM), (0, 1, 4, 2, 3))
    kT_all, vT_all, lfT_all = carried
    return (
        xp,
        xs.reshape(nb, t_new, d),
        heads_last(kT_all),
        heads_last(vT_all),
        jnp.transpose(lfT_all, (0, 1, 3, 2)),
        mem_view(mkT),
        mem_view(mvT),
        jnp.stack(kss).reshape(depth, nb, t_new, H_FOX, HEAD_DIM),
        jnp.stack(vss).reshape(depth, nb, t_new, H_FOX, HEAD_DIM),
        jnp.transpose(jnp.stack(lfss), (0, 2, 1)).reshape(depth, nb, t_new, H_FOX),
        jnp.stack(vgs).reshape(depth, nb, t_new, D_GMLP),
    )
```

```python
import functools

import jax
import jax.numpy as jnp
from jax import lax
from jax.experimental import pallas as pl
from jax.experimental.pallas import tpu as pltpu

HEAD_DIM = 64
H_FOX = 8
D_FOX = H_FOX * HEAD_DIM
G_GMLP = 4
D_GMLP = G_GMLP * HEAD_DIM
H_XATTN = 4
D_XATTN = H_XATTN * HEAD_DIM
D_MIX = D_FOX + D_GMLP + D_XATTN
N_MEM = 256
CHUNK = 128
PAGE_SIZE = 128
EPS = 1e-6
NEG = -1e30
SCALE = HEAD_DIM ** -0.5
LOG2E = 1.4426950408889634
LANES = 128
AUG = 2 * HEAD_DIM
D_REST = D_XATTN + 2 * D_GMLP + D_MIX
VMEM_LIMIT = 56 * 1024 * 1024
TM_PROJ = 512
TM_FINISH = 1024
FINISH_ROWS = 256
BQ = 512
QK_AHEAD = 2
DECODE_PAGES_PER_STAGE = 4

F32 = jnp.float32
BF16 = jnp.bfloat16

_NN = (((1,), (0,)), ((), ()))
_NT = (((1,), (1,)), ((), ()))
_TN = (((0,), (0,)), ((), ()))


def _dot(a, b, dims=_NN):
    return lax.dot_general(a, b, dims, preferred_element_type=F32)


def _split3(x):
    hi = x.astype(BF16).astype(F32)
    r = x - hi
    mid = r.astype(BF16).astype(F32)
    lo = (r - mid).astype(BF16).astype(F32)
    return hi, mid, lo


def _log_sigmoid(x):
    return jnp.minimum(x, 0.0) - jnp.log1p(jnp.exp(-jnp.abs(x)))


def _rms(x, gain):
    return x * lax.rsqrt(jnp.mean(x * x, axis=-1, keepdims=True) + EPS) * gain


def _cumsum_lanes(xT, tri):
    hi, mid, lo = _split3(xT)
    parts = jnp.concatenate([hi, mid, lo, jnp.zeros_like(hi)], axis=0).astype(BF16)
    cs = _dot(parts, tri)
    return cs[0:8] + cs[8:16] + cs[16:24]


def _proj_prompt_body(n_carried, x_ref, gpre_ref, wT_ref, wflT_ref, bf_ref, qg_ref, kg_ref, vgain_ref,
                      selq_ref, selk_ref, tri_ref, *refs):
    qTa_ref, ka_ref, vTa_ref, kT_ref, vT_ref, lfT_ref, rest_ref, carry_ref = refs[n_carried:]
    tm = x_ref.shape[1]
    if n_carried == 0:
        for ref in (kT_ref, vT_ref, lfT_ref):
            ref[1:] = jnp.zeros((ref.shape[0] - 1,) + ref.shape[1:], F32)
        kT_ref, vT_ref, lfT_ref = kT_ref.at[0], vT_ref.at[0], lfT_ref.at[0]

    @pl.when(pl.program_id(1) == 0)
    def _():
        carry_ref[...] = jnp.zeros_like(carry_ref)

    x = x_ref[0]
    h = _rms(x, gpre_ref[...]).astype(BF16)

    za = _dot(h, wT_ref[3 * D_FOX:3 * D_FOX + D_REST], _NT)
    rest_ref[0, :, 0:D_XATTN + D_GMLP] = za[:, 0:D_XATTN + D_GMLP]
    vg = za[:, D_XATTN + D_GMLP:D_XATTN + 2 * D_GMLP]
    rest_ref[0, :, D_XATTN + D_GMLP:D_XATTN + 2 * D_GMLP] = _rms(vg, vgain_ref[...])
    rest_ref[0, :, D_XATTN + 2 * D_GMLP:] = za[:, D_XATTN + 2 * D_GMLP:]

    zT = _dot(wT_ref[0:3 * D_FOX], h, _NT)
    flT = _dot(wflT_ref[...], h, _NT)
    lfT = _log_sigmoid(flT[0:8] + bf_ref[...])
    lfT_ref[0] = lfT

    cT = _cumsum_lanes(lfT, tri_ref[...]) + carry_ref[:, 0:1]
    carry_ref[...] = jnp.broadcast_to(cT[:, tm - 1:tm], carry_ref.shape)
    chi, cmid, clo = _split3(cT * LOG2E)
    cparts = jnp.concatenate([chi, cmid, clo, jnp.ones_like(chi)], axis=0).astype(BF16)

    qaugT = _dot(selq_ref[...], cparts)
    kaugT = _dot(selk_ref[...], cparts)
    row = lax.broadcasted_iota(jnp.int32, (HEAD_DIM, tm), 0)
    vaugT = jnp.where(row == 0, 1.0, 0.0).astype(F32)

    def head_norm(t, gain):
        return t * lax.rsqrt(jnp.mean(t * t, axis=0, keepdims=True) + EPS) * gain

    for hh in range(H_FOX):
        sl = slice(hh * HEAD_DIM, (hh + 1) * HEAD_DIM)
        qn = head_norm(zT[sl], qg_ref[...])
        qTa_ref[0, hh] = jnp.concatenate([qn, qaugT[sl]], axis=0).astype(BF16)
        kn = head_norm(zT[D_FOX + hh * HEAD_DIM:D_FOX + (hh + 1) * HEAD_DIM], kg_ref[...])
        kT_ref[0, hh] = kn
        ka_ref[0, hh] = jnp.concatenate([kn, kaugT[sl]], axis=0).T.astype(BF16)
        vh = zT[2 * D_FOX + hh * HEAD_DIM:2 * D_FOX + (hh + 1) * HEAD_DIM]
        vT_ref[0, hh] = vh
        vTa_ref[0, hh] = jnp.concatenate([vh, vaugT], axis=0).astype(BF16)


def _proj_prompt(x, layer, depth, wT, w, consts, tm, carried):
    b, s, d = x.shape
    n_t = s // tm
    full = lambda shape: pl.BlockSpec(shape, lambda bi, ti: (0,) * len(shape), pipeline_mode=pl.Buffered(1))
    in_specs = [
        pl.BlockSpec((1, tm, d), lambda bi, ti: (bi, ti, 0)),
        full((1, d)),
        pl.BlockSpec((None, 3 * D_FOX + D_REST, d), lambda bi, ti: (layer, 0, 0), pipeline_mode=pl.Buffered(1)),
        full(w['wflT'].shape),
        full((8, 1)), full((HEAD_DIM, 1)), full((HEAD_DIM, 1)), full((1, D_GMLP)),
        full(consts['selq'].shape), full(consts['selk'].shape), full(consts['tri'].shape),
    ]
    operands = [x, w['gpre'], wT, w['wflT'], w['bf'], w['qg'], w['kg'], w['vgain'],
                consts['selq'], consts['selk'], consts['tri']]
    aliases = {}
    for n, buf in enumerate(carried or ()):
        aliases[len(operands)] = 3 + n
        in_specs.append(pl.BlockSpec(memory_space=pl.ANY))
        operands.append(buf)
    lead = None if carried else depth
    lsel = layer if carried else 0
    out_shape = [
        jax.ShapeDtypeStruct((b, H_FOX, AUG, s), BF16),
        jax.ShapeDtypeStruct((b, H_FOX, s, AUG), BF16),
        jax.ShapeDtypeStruct((b, H_FOX, AUG, s), BF16),
        jax.ShapeDtypeStruct((depth, b, H_FOX, HEAD_DIM, s), F32),
        jax.ShapeDtypeStruct((depth, b, H_FOX, HEAD_DIM, s), F32),
        jax.ShapeDtypeStruct((depth, b, H_FOX, s), F32),
        jax.ShapeDtypeStruct((b, s, D_REST), F32),
    ]
    out_specs = [
        pl.BlockSpec((1, H_FOX, AUG, tm), lambda bi, ti: (bi, 0, 0, ti)),
        pl.BlockSpec((1, H_FOX, tm, AUG), lambda bi, ti: (bi, 0, ti, 0)),
        pl.BlockSpec((1, H_FOX, AUG, tm), lambda bi, ti: (bi, 0, 0, ti)),
        pl.BlockSpec((lead, 1, H_FOX, HEAD_DIM, tm), lambda bi, ti: (lsel, bi, 0, 0, ti)),
        pl.BlockSpec((lead, 1, H_FOX, HEAD_DIM, tm), lambda bi, ti: (lsel, bi, 0, 0, ti)),
        pl.BlockSpec((lead, 1, H_FOX, tm), lambda bi, ti: (lsel, bi, 0, ti)),
        pl.BlockSpec((1, tm, D_REST), lambda bi, ti: (bi, ti, 0)),
    ]
    return pl.pallas_call(
        functools.partial(_proj_prompt_body, len(aliases)),
        grid=(b, n_t),
        in_specs=in_specs, out_specs=out_specs, out_shape=out_shape,
        scratch_shapes=[pltpu.VMEM((8, LANES), F32)],
        input_output_aliases=aliases,
        compiler_params=pltpu.CompilerParams(
            dimension_semantics=("arbitrary", "arbitrary"), vmem_limit_bytes=VMEM_LIMIT),
        name="proj_prompt",
    )(*operands)


def _fox_tile(i, j, qTa_ref, ka_ref, vTa_ref, o_ref, m_ref, acc_ref, has_side, side_work):
    bq = qTa_ref.shape[3]
    bk = ka_ref.shape[2]

    @pl.when(j == 0)
    def _():
        m_ref[...] = jnp.full_like(m_ref, NEG)
        acc_ref[...] = jnp.zeros_like(acc_ref)

    def all_heads(diagonal, with_side):
        if diagonal:
            keep = (lax.broadcasted_iota(jnp.int32, (bk, bq), 0)
                    <= lax.broadcasted_iota(jnp.int32, (bk, bq), 1))
        side = side_work() if with_side else iter(())
        sTs = {h: _dot(ka_ref[0, h], qTa_ref[0, h]) for h in range(QK_AHEAD)}
        for hh in range(H_FOX):
            sT = sTs.pop(hh)
            if diagonal:
                sT = jnp.where(keep, sT, NEG)
            m_prev = m_ref[hh][0:1]
            m_new = jnp.maximum(m_prev, jnp.max(sT, axis=0, keepdims=True))
            alpha = jnp.exp2(m_prev - m_new)
            pT = jnp.exp2(sT - m_new).astype(BF16)
            if hh + QK_AHEAD < H_FOX:
                sTs[hh + QK_AHEAD] = _dot(ka_ref[0, hh + QK_AHEAD], qTa_ref[0, hh + QK_AHEAD])
            next(side, None)
            pv = _dot(vTa_ref[0, hh], pT)
            acc_ref[hh] = alpha * acc_ref[hh] + pv
            m_ref[hh] = jnp.broadcast_to(m_new, (8, bq))
        for _ in side:
            pass

    def finalize():
        lane = lax.broadcasted_iota(jnp.int32, (bq, LANES), 1)
        for pair in range(H_FOX // 2):
            a0 = acc_ref[2 * pair]
            a1 = acc_ref[2 * pair + 1]
            o0 = (a0 * (1.0 / a0[HEAD_DIM:HEAD_DIM + 1])).T
            o1 = (a1 * (1.0 / a1[HEAD_DIM:HEAD_DIM + 1])).T
            o_ref[0, :, pair * LANES:(pair + 1) * LANES] = jnp.where(
                lane < HEAD_DIM, o0, pltpu.roll(o1, HEAD_DIM, 1))

    for with_side in (False, True):
        side_now = has_side if with_side else jnp.logical_not(has_side)

        @pl.when((j < i) & side_now)
        def _():
            all_heads(False, with_side)

        @pl.when((j == i) & side_now)
        def _():
            all_heads(True, with_side)
            finalize()


def _mem_kv_body(mem_ref, gain_ref, wT_ref, mkT_ref, mvT_ref, mkbd_ref, mvbd_ref):
    hm = _rms(mem_ref[0], gain_ref[0]).astype(BF16)
    kvT = _dot(wT_ref[0], hm, _NT)
    mkT = kvT[:D_XATTN]
    mvT = kvT[D_XATTN:]
    mkT_ref[0, 0] = mkT
    mvT_ref[0, 0] = mvT
    mkbd_ref[...] = jnp.zeros_like(mkbd_ref)
    mvbd_ref[...] = jnp.zeros_like(mvbd_ref)
    for hh in range(H_XATTN):
        r = slice(hh * HEAD_DIM, (hh + 1) * HEAD_DIM)
        c = slice(hh * N_MEM, (hh + 1) * N_MEM)
        mkbd_ref[0, 0, r, c] = mkT[r].astype(BF16)
        mvbd_ref[0, 0, r, c] = mvT[r].astype(BF16)


def _mem_kv(mem, mem_norm, wmT):
    b, n_mem, d = mem.shape
    depth = wmT.shape[0]
    blk = lambda shape: pl.BlockSpec((1, 1) + shape, lambda l, bi: (l, bi, 0, 0))
    return pl.pallas_call(
        _mem_kv_body,
        grid=(depth, b),
        in_specs=[
            pl.BlockSpec((1, n_mem, d), lambda l, bi: (bi, 0, 0)),
            pl.BlockSpec((1, 1, d), lambda l, bi: (l, 0, 0)),
            pl.BlockSpec((1, 2 * D_XATTN, d), lambda l, bi: (l, 0, 0)),
        ],
        out_specs=[blk((D_XATTN, n_mem)), blk((D_XATTN, n_mem)),
                   blk((D_XATTN, H_XATTN * n_mem)), blk((D_XATTN, H_XATTN * n_mem))],
        out_shape=[
            jax.ShapeDtypeStruct((depth, b, D_XATTN, n_mem), F32),
            jax.ShapeDtypeStruct((depth, b, D_XATTN, n_mem), F32),
            jax.ShapeDtypeStruct((depth, b, D_XATTN, H_XATTN * n_mem), BF16),
            jax.ShapeDtypeStruct((depth, b, D_XATTN, H_XATTN * n_mem), BF16),
        ],
        compiler_params=pltpu.CompilerParams(
            dimension_semantics=("arbitrary", "arbitrary"), vmem_limit_bytes=VMEM_LIMIT),
        name="mem_kv",
    )(mem, mem_norm.reshape(depth, 1, d), wmT)


def _merge_and_project(x, fo, go, xo, gate, bn_ref, wout_ref, gpost_ref):
    merged = jnp.concatenate([
        _rms(fo, bn_ref[:, 0:D_FOX]),
        _rms(go, bn_ref[:, D_FOX:D_FOX + D_GMLP]),
        _rms(xo, bn_ref[:, D_FOX + D_GMLP:]),
    ], axis=-1) * (gate * jax.nn.sigmoid(gate))
    y = _dot(merged.astype(BF16), wout_ref[...])
    return x + _rms(y, gpost_ref[...])


def _finish_rows(rows, w_tril, grp, x_ref, o_ref, rest_ref, mkbd_ref, mvbd_ref, bsp_ref, bn_ref, wout_ref,
                 gpost_ref, y_ref):
    n = rows.stop - rows.start
    rest = rest_ref[0, rows]
    qx = rest[:, 0:D_XATTN]
    u = rest[:, D_XATTN:D_XATTN + D_GMLP]
    vg = rest[:, D_XATTN + D_GMLP:D_XATTN + 2 * D_GMLP]
    gate = rest[:, D_XATTN + 2 * D_GMLP:]
    malls = [_dot(w_tril, vg[c * CHUNK:(c + 1) * CHUNK].astype(BF16)) for c in range(n // CHUNK)]
    s = _dot(qx.astype(BF16), mkbd_ref[0, 0]) * SCALE
    yield
    gos = []
    for c, mall in enumerate(malls):
        mixed = mall[0:CHUNK]
        for g in range(1, G_GMLP):
            mixed = jnp.where(grp == g, mall[g * CHUNK:(g + 1) * CHUNK], mixed)
        gos.append(u[c * CHUNK:(c + 1) * CHUNK] * (mixed + bsp_ref[...]))
    go = jnp.concatenate(gos, axis=0)
    ps = []
    for hh in range(H_XATTN):
        sh = s[:, hh * N_MEM:(hh + 1) * N_MEM]
        e = jnp.exp(sh - jnp.max(sh, axis=-1, keepdims=True))
        ps.append(e / jnp.sum(e, axis=-1, keepdims=True))
    p = jnp.concatenate(ps, axis=-1).astype(BF16)
    yield
    xo = _dot(p, mvbd_ref[0, 0], _NT)
    y_ref[0, rows] = _merge_and_project(x_ref[0, rows], o_ref[0, rows], go, xo, gate, bn_ref, wout_ref, gpost_ref)


def _finish_prompt_body(x_ref, o_ref, rest_ref, mkbd_ref, mvbd_ref, wsp_ref, bsp_ref, bn_ref, wout_ref,
                        gpost_ref, y_ref):
    tm = x_ref.shape[1]
    wi = lax.broadcasted_iota(jnp.int32, wsp_ref.shape, 0) % CHUNK
    wj = lax.broadcasted_iota(jnp.int32, wsp_ref.shape, 1)
    w_tril = jnp.where(wj <= wi, wsp_ref[...], 0.0).astype(BF16)
    grp = lax.broadcasted_iota(jnp.int32, (CHUNK, D_GMLP), 1) // HEAD_DIM
    waiting = [_finish_rows(slice(r0, r0 + FINISH_ROWS), w_tril, grp, x_ref, o_ref, rest_ref, mkbd_ref, mvbd_ref,
                            bsp_ref, bn_ref, wout_ref, gpost_ref, y_ref) for r0 in range(0, tm, FINISH_ROWS)]
    live = []
    while waiting or live:
        if waiting:
            live.append(waiting.pop(0))
        for g in list(live):
            if next(g, StopIteration) is StopIteration:
                live.remove(g)


def _finish_prompt(x, o, rest, mkbd, mvbd, layer, w, tm):
    b, s, d = x.shape
    full = lambda shape: pl.BlockSpec(shape, lambda bi, ti: (0,) * len(shape))
    tile = lambda width: pl.BlockSpec((1, tm, width), lambda bi, ti: (bi, ti, 0))
    kvspec = pl.BlockSpec((1, 1, D_XATTN, H_XATTN * N_MEM), lambda bi, ti: (layer, bi, 0, 0))
    return pl.pallas_call(
        _finish_prompt_body,
        grid=(b, s // tm),
        in_specs=[tile(d), tile(D_FOX), tile(D_REST), kvspec, kvspec,
                  full((G_GMLP * CHUNK, CHUNK)), full((CHUNK, D_GMLP)), full((1, D_MIX)),
                  full((D_MIX, d)), full((1, d))],
        out_specs=tile(d),
        out_shape=jax.ShapeDtypeStruct((b, s, d), F32),
        compiler_params=pltpu.CompilerParams(
            dimension_semantics=("arbitrary", "arbitrary"), vmem_limit_bytes=VMEM_LIMIT),
        name="finish_prompt",
    )(x, o, rest, mkbd, mvbd, w['wsp'], w['bsp'], w['bn'], w['wout'], w['gpost'])


def _proj_sample_body(x_ref, gpre_ref, wT_ref, wflT_ref, bf_ref, qg_ref, kgrow_ref, vgain_ref, gsum_ref,
                      q_ref, k_ref, v_ref, lfT_ref, rest_ref):
    h = _rms(x_ref[...], gpre_ref[...]).astype(BF16)
    z = _dot(h, wT_ref[...], _NT)

    def head_norm(t, gain):
        t2 = t * t
        t2h = t2.astype(BF16)
        t2l = (t2 - t2h.astype(F32)).astype(BF16)
        msq = (_dot(t2h, gsum_ref[...]) + _dot(t2l, gsum_ref[...])) * (1.0 / HEAD_DIM)
        return t * lax.rsqrt(msq + EPS) * gain

    q_ref[...] = head_norm(z[:, 0:D_FOX], qg_ref[...]).astype(BF16)
    k_ref[...] = head_norm(z[:, D_FOX:2 * D_FOX], kgrow_ref[...])
    v_ref[...] = z[:, 2 * D_FOX:3 * D_FOX]
    flT = _dot(wflT_ref[...], h, _NT)
    lfT_ref[...] = _log_sigmoid(flT[0:8] + bf_ref[...])
    base = 3 * D_FOX
    rest_ref[:, 0:D_XATTN + D_GMLP] = z[:, base:base + D_XATTN + D_GMLP]
    vg = z[:, base + D_XATTN + D_GMLP:base + D_XATTN + 2 * D_GMLP]
    rest_ref[:, D_XATTN + D_GMLP:D_XATTN + 2 * D_GMLP] = _rms(vg, vgain_ref[...])
    rest_ref[:, D_XATTN + 2 * D_GMLP:] = z[:, base + D_XATTN + 2 * D_GMLP:]


def _proj_sample(x, layer, wT, w, consts):
    n, d = x.shape
    full = lambda shape: pl.BlockSpec(shape, lambda i: (0,) * len(shape))
    return pl.pallas_call(
        _proj_sample_body,
        grid=(1,),
        in_specs=[full((n, d)), full((1, d)),
                  pl.BlockSpec((None, 3 * D_FOX + D_REST, d), lambda i: (layer, 0, 0)),
                  full(w['wflT'].shape), full((8, 1)),
                  full((1, D_FOX)), full((1, D_FOX)), full((1, D_GMLP)), full(consts['gsum'].shape)],
        out_specs=[full((n, D_FOX)), full((n, D_FOX)), full((n, D_FOX)), full((8, n)), full((n, D_REST))],
        out_shape=[
            jax.ShapeDtypeStruct((n, D_FOX), BF16),
            jax.ShapeDtypeStruct((n, D_FOX), F32),
            jax.ShapeDtypeStruct((n, D_FOX), F32),
            jax.ShapeDtypeStruct((8, n), F32),
            jax.ShapeDtypeStruct((n, D_REST), F32),
        ],
        compiler_params=pltpu.CompilerParams(
            dimension_semantics=("arbitrary",), vmem_limit_bytes=VMEM_LIMIT),
        name="proj_sample",
    )(x, w['gpre'], wT, w['wflT'], w['bf'], w['qgrow'], w['kgrow'], w['vgain'], consts['gsum'])


def _expand_rows(x4, width):
    t = x4.shape[0]
    rep = jnp.concatenate([jnp.broadcast_to(x4[i:i + 1], (8, width)) for i in range(t)], axis=0)
    rowh = lax.broadcasted_iota(jnp.int32, (8 * t, width), 0) % 8
    colh = lax.broadcasted_iota(jnp.int32, (8 * t, width), 1) // HEAD_DIM
    return jnp.where(rowh == colh, rep, jnp.zeros_like(rep))


def _collapse_rows(x, t):
    width = x.shape[1]
    rowh = lax.broadcasted_iota(jnp.int32, x.shape, 0) % 8
    colh = lax.broadcasted_iota(jnp.int32, x.shape, 1) // HEAD_DIM
    xm = jnp.where(rowh == colh, x, 0.0)
    return jnp.concatenate([jnp.sum(xm[i * 8:(i + 1) * 8], axis=0, keepdims=True) for i in range(t)], axis=0)


def _decode_row(b, t_new, q_ref, knew_ref, vnew_ref, lfT_ref, rest_ref, mkT_ref, mvT_ref, tri_ref,
                k_refs, v_refs, lf_refs, fo_ref, xo_ref):
    n_pages = len(k_refs)
    nrow = 8 * t_new
    n_tok = lfT_ref.shape[1]

    qbd = _expand_rows(q_ref[...].astype(F32), D_FOX).astype(BF16)

    lf_all = jnp.concatenate([lf_refs[p][...] for p in range(n_pages)], axis=0)
    hi, mid, lo = _split3(lf_all)
    cs = _dot(jnp.concatenate([hi, mid, lo], axis=0).astype(BF16), tri_ref[...])
    np8 = 8 * n_pages
    c_in = cs[0:np8] + cs[np8:2 * np8] + cs[2 * np8:3 * np8]
    carry = jnp.zeros((8, 1), F32)
    cks = []
    for p in range(n_pages):
        cks.append(c_in[8 * p:8 * p + 8] + carry)
        carry = carry + c_in[8 * p:8 * p + 8, PAGE_SIZE - 1:PAGE_SIZE]
    jj = lax.broadcasted_iota(jnp.int32, (n_tok, LANES), 0)
    tt = lax.broadcasted_iota(jnp.int32, (n_tok, LANES), 1)
    sel = ((jj >= b * t_new) & (jj <= b * t_new + tt) & (tt < t_new)).astype(BF16)
    cn = _cumsum_lanes(lfT_ref[...], sel) + carry
    cq = jnp.concatenate([cn[:, i:i + 1] for i in range(t_new)], axis=0)

    ss = []
    for p in range(n_pages):
        ss.append(_dot(qbd, k_refs[p][...].astype(BF16)) + (cq - jnp.concatenate([cks[p]] * t_new, axis=0)))
        if p % DECODE_PAGES_PER_STAGE == DECODE_PAGES_PER_STAGE - 1 and p + 1 < n_pages:
            yield
    pad = jnp.zeros((16 - t_new, D_FOX), F32)
    knew = jnp.concatenate([knew_ref[...], pad], axis=0).astype(BF16)
    vnew = jnp.concatenate([vnew_ref[...], pad], axis=0).astype(BF16)
    sn = _dot(qbd, knew, _NT) + (cq - jnp.concatenate([cn[:, 0:16]] * t_new, axis=0))
    qt = lax.broadcasted_iota(jnp.int32, (nrow, 16), 0) // 8
    kt = lax.broadcasted_iota(jnp.int32, (nrow, 16), 1)
    sn = jnp.where(kt <= qt, sn, NEG)

    smax = ss[0]
    for s in ss[1:]:
        smax = jnp.maximum(smax, s)
    m = jnp.maximum(jnp.max(smax, axis=-1, keepdims=True), jnp.max(sn, axis=-1, keepdims=True))
    en = jnp.exp(sn - m)
    acc = _dot(en.astype(BF16), vnew)
    esum = jnp.zeros((nrow, PAGE_SIZE), F32)
    for p in range(n_pages):
        if p % DECODE_PAGES_PER_STAGE == 0:
            yield
        e = jnp.exp(ss[p] - m)
        esum = esum + e
        acc = acc + _dot(e.astype(BF16), v_refs[p][...].astype(BF16), _NT)
    l = jnp.sum(esum, axis=-1, keepdims=True) + jnp.sum(en, axis=-1, keepdims=True)
    fo_ref[...] = _collapse_rows(acc * (1.0 / l), t_new)

    qx = rest_ref[:, 0:D_XATTN]
    qxbd = _expand_rows(qx, D_XATTN).astype(BF16)
    sx = _dot(qxbd, mkT_ref[...].astype(BF16)) * SCALE
    ex = jnp.exp(sx - jnp.max(sx, axis=-1, keepdims=True))
    px = (ex / jnp.sum(ex, axis=-1, keepdims=True)).astype(BF16)
    xo_ref[...] = _collapse_rows(_dot(px, mvT_ref[...].astype(BF16), _NT), t_new)


def _attend_body(layer, n_pages, t_new, rows_per_b, n_rows, qi_ref, kj_ref, pt_ref, qTa_ref, ka_ref, vTa_ref,
                 q_ref, knew_ref, vnew_ref, lfT_ref, rest_ref, mkT_ref, mvT_ref, tri_ref, lfc_ref, kc_ref, vc_ref,
                 o_ref, fo_ref, xo_ref, m_ref, acc_ref, kbuf, vbuf, sem):
    t = pl.program_id(1)
    row = pl.program_id(0) * rows_per_b + t
    has_row = t < rows_per_b

    def page_copies(r, slot):
        copies = []
        for p in range(n_pages):
            page = pt_ref[r, p]
            copies.append(pltpu.make_async_copy(kc_ref.at[layer, page], kbuf.at[slot, p], sem.at[slot, 0]))
            copies.append(pltpu.make_async_copy(vc_ref.at[layer, page], vbuf.at[slot, p], sem.at[slot, 1]))
        return copies

    @pl.when(row == 0)
    def _():
        for c in page_copies(0, 0):
            c.start()

    @pl.when(has_row & (row + 1 < n_rows))
    def _():
        for c in page_copies(row + 1, (row + 1) % 2):
            c.start()

    @pl.when(has_row)
    def _():
        for c in page_copies(row, row % 2):
            c.wait()

    def sample_row():
        slot = row % 2
        k_pages = [kbuf.at[slot, p] for p in range(n_pages)]
        v_pages = [vbuf.at[slot, p] for p in range(n_pages)]
        lf_pages = [lfc_ref.at[pt_ref[row, p]] for p in range(n_pages)]
        return _decode_row(row, t_new, q_ref, knew_ref, vnew_ref, lfT_ref, rest_ref, mkT_ref, mvT_ref, tri_ref,
                           k_pages, v_pages, lf_pages, fo_ref, xo_ref)

    _fox_tile(qi_ref[t], kj_ref[t], qTa_ref, ka_ref, vTa_ref, o_ref, m_ref, acc_ref, has_row, sample_row)


def _attend(layer, qTa, ka, vTa, bq, page_table, q, knew, vnew, lfT, rest, kc, vc, lfc, mkc, mvc, tri_page):
    b, _, _, s = qTa.shape
    nb, t_new, _ = q.shape
    n_pages = page_table.shape[1]
    nq = s // bq
    pairs = [(i, j) for i in range(nq) for j in range(i + 1)]
    qi = jnp.array([p[0] for p in pairs], jnp.int32)
    kj = jnp.array([p[1] for p in pairs], jnp.int32)
    assert nb % b == 0 and nb // b <= len(pairs), (nb, b, len(pairs))
    rows_per_b = nb // b
    srow = lambda bi, t: bi * rows_per_b + jnp.minimum(t, rows_per_b - 1)

    row = lambda width: pl.BlockSpec((None, t_new, width), lambda bi, t, qi, kj, pt: (srow(bi, t), 0, 0))
    mem = pl.BlockSpec((None, None, D_XATTN, N_MEM), lambda bi, t, qi, kj, pt: (layer, srow(bi, t), 0, 0))
    whole = lambda a: pl.BlockSpec(a.shape, lambda bi, t, qi, kj, pt: (0,) * a.ndim)
    lf_table = pl.BlockSpec((None,) + lfc.shape[1:], lambda bi, t, qi, kj, pt: (layer, 0, 0, 0),
                            pipeline_mode=pl.Buffered(1))
    hbm = pl.BlockSpec(memory_space=pl.ANY)
    in_specs = [
        pl.BlockSpec((1, H_FOX, AUG, bq), lambda bi, t, qi, kj, pt: (bi, 0, 0, qi[t])),
        pl.BlockSpec((1, H_FOX, bq, AUG), lambda bi, t, qi, kj, pt: (bi, 0, kj[t], 0)),
        pl.BlockSpec((1, H_FOX, AUG, bq), lambda bi, t, qi, kj, pt: (bi, 0, 0, kj[t])),
        row(D_FOX), row(D_FOX), row(D_FOX), whole(lfT), row(D_REST), mem, mem, whole(tri_page),
        lf_table, hbm, hbm]
    page_buf = pltpu.VMEM((2, n_pages, D_FOX, PAGE_SIZE), F32)
    return pl.pallas_call(
        functools.partial(_attend_body, layer, n_pages, t_new, rows_per_b, nb),
        grid_spec=pltpu.PrefetchScalarGridSpec(
            num_scalar_prefetch=3,
            grid=(b, len(pairs)),
            in_specs=in_specs,
            out_specs=[pl.BlockSpec((1, bq, D_FOX), lambda bi, t, qi, kj, pt: (bi, qi[t], 0)),
                       row(D_FOX), row(D_XATTN)],
            scratch_shapes=[pltpu.VMEM((H_FOX, 8, bq), F32), pltpu.VMEM((H_FOX, AUG, bq), F32),
                            page_buf, page_buf, pltpu.SemaphoreType.DMA((2, 2))]),
        out_shape=[jax.ShapeDtypeStruct((b, s, D_FOX), F32),
                   jax.ShapeDtypeStruct((nb, t_new, D_FOX), F32),
                   jax.ShapeDtypeStruct((nb, t_new, D_XATTN), F32)],
        compiler_params=pltpu.CompilerParams(
            dimension_semantics=("arbitrary", "arbitrary"), vmem_limit_bytes=VMEM_LIMIT),
        name="attend",
    )(qi, kj, page_table, qTa, ka, vTa, q, knew, vnew, lfT, rest, mkc, mvc, tri_page, lfc, kc, vc)


def _finish_sample_body(t_new, wsp_ref, bsp_ref, x_ref, fo_ref, xo_ref, rest_ref, bn_ref, wout_ref,
                        gpost_ref, y_ref):
    n = x_ref.shape[0]
    u = rest_ref[:, D_XATTN:D_XATTN + D_GMLP]
    vg = rest_ref[:, D_XATTN + D_GMLP:D_XATTN + 2 * D_GMLP]
    gate = rest_ref[:, D_XATTN + 2 * D_GMLP:]

    pos = lax.broadcasted_iota(jnp.int32, (n, D_GMLP), 0) % t_new
    grp = lax.broadcasted_iota(jnp.int32, (n, D_GMLP), 1) // HEAD_DIM
    mixed = jnp.zeros((n, D_GMLP), F32)
    for g in range(G_GMLP):
        for i in range(t_new):
            mixed = jnp.where((grp == g) & (pos == i), bsp_ref[g, i], mixed)
    for k in range(t_new):
        coef = jnp.zeros((n, D_GMLP), F32)
        for g in range(G_GMLP):
            for i in range(k, t_new):
                coef = jnp.where((grp == g) & (pos == i), wsp_ref[g, i * t_new + i - k], coef)
        shifted = vg if k == 0 else pltpu.roll(vg, k, 0)
        mixed = mixed + coef * shifted
    go = u * mixed
    y_ref[...] = _merge_and_project(x_ref[...], fo_ref[...], go, xo_ref[...], gate, bn_ref, wout_ref,
                                    gpost_ref)


def _finish_sample(x, fo, xo, rest, w_small, b_small, w, t_new):
    n, d = x.shape
    full = lambda shape: pl.BlockSpec(shape, lambda i: (0,) * len(shape))
    smem = pl.BlockSpec(memory_space=pltpu.SMEM)
    return pl.pallas_call(
        functools.partial(_finish_sample_body, t_new),
        grid=(1,),
        in_specs=[smem, smem, full((n, d)), full((n, D_FOX)), full((n, D_XATTN)), full((n, D_REST)),
                  full((1, D_MIX)), full((D_MIX, d)), full((1, d))],
        out_specs=full((n, d)),
        out_shape=jax.ShapeDtypeStruct((n, d), F32),
        compiler_params=pltpu.CompilerParams(
            dimension_semantics=("arbitrary",), vmem_limit_bytes=VMEM_LIMIT),
        name="finish_sample",
    )(w_small, b_small, x, fo, xo, rest, w['bn'], w['wout'], w['gpost'])


def _constants(tm):
    f = jnp.arange(D_FOX)
    gsum = (f[:, None] // HEAD_DIM == f[None, :] // HEAD_DIM).astype(BF16)
    r = jnp.arange(32)
    fh, fo = f // HEAD_DIM, f % HEAD_DIM
    selq = jnp.where((fo[:, None] < 3) & (r[None, :] == fo[:, None] * 8 + fh[:, None]), 1.0,
                     jnp.where((fo[:, None] >= 3) & (fo[:, None] < 6) & (r[None, :] == 24), 1.0, 0.0)).astype(BF16)
    selk = jnp.where((fo[:, None] < 3) & (r[None, :] == 24), 1.0,
                     jnp.where((fo[:, None] >= 3) & (fo[:, None] < 6)
                               & (r[None, :] == (fo[:, None] - 3) * 8 + fh[:, None]), -1.0, 0.0)).astype(BF16)
    t = jnp.arange(tm)
    tri = (t[:, None] <= t[None, :]).astype(BF16)
    p = jnp.arange(PAGE_SIZE)
    tri_page = (p[:, None] <= p[None, :]).astype(BF16)
    return dict(gsum=gsum, selq=selq, selk=selk, tri=tri, tri_page=tri_page)


def _layer_weights(l, norm_pre, w_in, b_forget, q_norm, k_norm, gmlp_v_norm, w_spatial, b_spatial,
                   branch_norm, w_out, norm_post):
    d = w_in.shape[1]
    fl_cols = w_in[l][:, 3 * D_FOX + D_REST:]
    wflT = jnp.concatenate([fl_cols.T, jnp.zeros((8, d), F32)], axis=0).astype(BF16)
    return dict(
        gpre=norm_pre[l].reshape(1, d),
        wflT=wflT,
        bf=b_forget[l].reshape(H_FOX, 1),
        qg=(q_norm[l] * (SCALE * LOG2E)).reshape(HEAD_DIM, 1),
        qgrow=(jnp.tile(q_norm[l], H_FOX) * SCALE).reshape(1, D_FOX),
        kg=k_norm[l].reshape(HEAD_DIM, 1),
        kgrow=jnp.tile(k_norm[l], H_FOX).reshape(1, D_FOX),
        vgain=gmlp_v_norm[l].reshape(1, D_GMLP),
        wsp=w_spatial[l].reshape(G_GMLP * CHUNK, CHUNK),
        bsp=jnp.repeat(b_spatial[l].T, HEAD_DIM, axis=1),
        bn=branch_norm[l].reshape(1, D_MIX),
        wout=w_out[l].astype(BF16),
        gpost=norm_post[l].reshape(1, d),
    )


def kernel(x_prompt, x_sample, mem_prompt, cache_fox_k, cache_fox_v, cache_fox_lf, cache_mem_k, cache_mem_v,
           page_table, norm_pre, w_in, b_forget, q_norm, k_norm, gmlp_v_norm, w_spatial, b_spatial, mem_norm,
           w_mem_kv, branch_norm, w_out, norm_post):
    depth = w_in.shape[0]
    b, s, d = x_prompt.shape
    nb, t_new, _ = x_sample.shape
    n_pool = cache_fox_k.shape[1]
    tm_proj = min(TM_PROJ, s)
    tm_finish = min(TM_FINISH, s)
    bq = min(BQ, s)
    consts = _constants(tm_proj)

    kc = jnp.transpose(cache_fox_k, (0, 1, 3, 4, 2)).reshape(depth, n_pool, D_FOX, PAGE_SIZE)
    vc = jnp.transpose(cache_fox_v, (0, 1, 3, 4, 2)).reshape(depth, n_pool, D_FOX, PAGE_SIZE)
    lfc = jnp.transpose(cache_fox_lf, (0, 1, 3, 2))
    mkc = jnp.transpose(cache_mem_k, (0, 1, 3, 4, 2)).reshape(depth, nb, D_XATTN, N_MEM)
    mvc = jnp.transpose(cache_mem_v, (0, 1, 3, 4, 2)).reshape(depth, nb, D_XATTN, N_MEM)

    wmT = jnp.transpose(w_mem_kv, (0, 2, 1)).astype(BF16)
    mkT, mvT, mkbd, mvbd = _mem_kv(mem_prompt, mem_norm, wmT)

    wT = jnp.transpose(w_in, (0, 2, 1)).astype(BF16)

    xp = x_prompt
    xs = x_sample.reshape(nb * t_new, d)
    carried = None
    kss, vss, lfss, vgs = [], [], [], []
    for l in range(depth):
        w = _layer_weights(l, norm_pre, w_in, b_forget, q_norm, k_norm, gmlp_v_norm, w_spatial, b_spatial,
                           branch_norm, w_out, norm_post)
        qTa, ka, vTa, kT_all, vT_all, lfT_all, rest = _proj_prompt(xp, l, depth, wT, w, consts, tm_proj, carried)
        carried = (kT_all, vT_all, lfT_all)
        q_s, k_s, v_s, lfT_s, rest_s = _proj_sample(xs, l, wT, w, consts)
        r3 = lambda a: a.reshape(nb, t_new, a.shape[-1])
        o, fo, xo = _attend(l, qTa, ka, vTa, bq, page_table, r3(q_s), r3(k_s), r3(v_s), lfT_s, r3(rest_s),
                            kc, vc, lfc, mkc, mvc, consts['tri_page'])
        xp = _finish_prompt(xp, o, rest, mkbd, mvbd, l, w, tm_finish)
        w_small = w_spatial[l][:, :t_new, :t_new].reshape(G_GMLP, t_new * t_new)
        b_small = b_spatial[l][:, :t_new]
        xs = _finish_sample(xs, fo.reshape(nb * t_new, D_FOX), xo.reshape(nb * t_new, D_XATTN), rest_s,
                            w_small, b_small, w, t_new)
        kss.append(k_s); vss.append(v_s); lfss.append(lfT_s)
        vgs.append(rest_s[:, D_XATTN + D_GMLP:D_XATTN + 2 * D_GMLP])

    heads_last = lambda t: jnp.transpose(t, (0, 1, 4, 2, 3))
    mem_view = lambda m: jnp.transpose(m.reshape(depth, b, H_XATTN, HEAD_DIM, N_MEM), (0, 1, 4, 2, 3))
    kT_all, vT_all, lfT_all = carried
    return (
        xp,
        xs.reshape(nb, t_new, d),
        heads_last(kT_all),
        heads_last(vT_all),
        jnp.transpose(lfT_all, (0, 1, 3, 2)),
        mem_view(mkT),
        mem_view(mvT),
        jnp.stack(kss).reshape(depth, nb, t_new, H_FOX, HEAD_DIM),
        jnp.stack(vss).reshape(depth, nb, t_new, H_FOX, HEAD_DIM),
        jnp.transpose(jnp.stack(lfss), (0, 2, 1)).reshape(depth, nb, t_new, H_FOX),
        jnp.stack(vgs).reshape(depth, nb, t_new, D_GMLP),
    )
```
